```python
import math
import jax, jax.numpy as jnp
from jax import lax
import numpy as np

D_MODEL = 4096
BATCH = 1
SEQ = 8192
DEPTH = 1
DEC_BATCH = 32
DEC_SEQ = 4
PAST_LEN = 8192
PAGE_SIZE = 128

D_INNER = D_MODEL
SSM_HEAD_DIM = 64
N_SSM_HEADS = D_INNER // SSM_HEAD_DIM
N_SSM_GROUPS = 8
D_STATE = 128
CONV_W = 4
CONV_DIM = D_INNER + 2 * N_SSM_GROUPS * D_STATE
SSD_CHUNK = 256
SB_HEAD_DIM = 128
N_SB_HEADS = 16
SB_WIDTH = N_SB_HEADS * SB_HEAD_DIM
Q_BLOCK = 128
SB_MIN_LOG2_SPAN = 2.0
SB_MAX_LOG2_SPAN = 14.0
PEER_KEYS = 128
PEER_EXPERTS = PEER_KEYS * PEER_KEYS
PEER_HEADS = 8
PEER_QDIM = 256
PEER_TOPK = 16
PEER_TOK_BLOCK = 128
IN_SIZES = (D_INNER, CONV_DIM, N_SSM_HEADS, SB_WIDTH, SB_WIDTH, SB_WIDTH, D_MODEL, D_MODEL)
IN_COLS = sum(IN_SIZES)
EPS = 1e-6

kernel_name = 'hybrid_ssd_stickbreak_peer_step'


def rmsnorm(x, w):
    xf = x.astype(jnp.float32)
    r = xf * lax.rsqrt(jnp.mean(xf * xf, axis=-1, keepdims=True) + EPS)
    return (r * w).astype(x.dtype)


def gated_rmsnorm(y, z, w):
    u = (y * jax.nn.silu(z)).astype(jnp.float32)
    u = u.reshape(y.shape[:-1] + (N_SSM_GROUPS, D_INNER // N_SSM_GROUPS))
    u = u * lax.rsqrt(jnp.mean(u * u, axis=-1, keepdims=True) + EPS)
    return (u.reshape(y.shape) * w).astype(y.dtype)


def ssd_scan(xs, dt, a, bm, cm, h0):
    b, t, h, p = xs.shape
    g, n = bm.shape[2], bm.shape[3]
    r = h // g
    l = min(SSD_CHUNK, t)
    nc = -(-t // l)
    pad = nc * l - t
    padt = lambda u: jnp.pad(u, [(0, 0), (0, pad)] + [(0, 0)] * (u.ndim - 2))
    xc = padt(xs).reshape(b, nc, l, g, r, p).astype(jnp.float32)
    dtc = padt(dt).reshape(b, nc, l, g, r)
    bc = padt(bm).reshape(b, nc, l, g, n).astype(jnp.float32)
    cc = padt(cm).reshape(b, nc, l, g, n).astype(jnp.float32)
    acs = jnp.cumsum(dtc * a.reshape(g, r).astype(jnp.float32), axis=2)
    xdt = xc * dtc[..., None]
    seg = acs[:, :, :, None] - acs[:, :, None, :]
    causal = jnp.tril(jnp.ones((l, l), bool))[:, :, None, None]
    decay = jnp.exp(jnp.where(causal, seg, -jnp.inf))
    cb = jnp.einsum('bclgn,bcsgn->bclsg', cc, bc)
    y_diag = jnp.einsum('bclsg,bclsgr,bcsgrp->bclgrp', cb, decay, xdt)
    dec_end = jnp.exp(acs[:, :, -1:] - acs)
    states = jnp.einsum('bclgn,bclgr,bclgrp->bcgrpn', bc, dec_end, xdt)
    chunk_decay = jnp.exp(acs[:, :, -1])

    def step(hs, inp):
        st, cd = inp
        return hs * cd[..., None, None] + st, hs

    h0r = h0.reshape(b, g, r, p, n).astype(jnp.float32)
    h_final, h_prev = lax.scan(step, h0r, (jnp.moveaxis(states, 1, 0), jnp.moveaxis(chunk_decay, 1, 0)))
    h_prev = jnp.moveaxis(h_prev, 0, 1)
    y_off = jnp.einsum('bclgn,bcgrpn,bclgr->bclgrp', cc, h_prev, jnp.exp(acs))
    y = (y_diag + y_off).reshape(b, nc * l, h, p)[:, :t]
    return y.astype(xs.dtype), h_final.reshape(b, h, p, n)


def mamba_branch(z, xbc, dt_raw, conv_buf, ssm_state, conv_w, conv_b, dt_bias, a_log, d_skip, ssm_norm_w):
    b, t, _ = xbc.shape
    ext = jnp.concatenate([conv_buf.astype(xbc.dtype), xbc], axis=1)
    conv = conv_b + sum(ext[:, k:k + t] * conv_w[k] for k in range(CONV_W))
    new_conv = ext[:, t:]
    xbc_c = jax.nn.silu(conv)
    xs, bm, cm = jnp.split(xbc_c, [D_INNER, D_INNER + N_SSM_GROUPS * D_STATE], axis=-1)
    xs = xs.reshape(b, t, N_SSM_HEADS, SSM_HEAD_DIM)
    bm = bm.reshape(b, t, N_SSM_GROUPS, D_STATE)
    cm = cm.reshape(b, t, N_SSM_GROUPS, D_STATE)
    dt = jax.nn.softplus(dt_raw.astype(jnp.float32) + dt_bias.astype(jnp.float32))
    a = -jnp.exp(a_log.astype(jnp.float32))
    y, new_ssm = ssd_scan(xs, dt, a, bm, cm, ssm_state)
    y = (y + xs * d_skip[:, None]).reshape(b, t, D_INNER)
    return gated_rmsnorm(y, z, ssm_norm_w), new_conv, new_ssm


def sb_block(qb, q_start, k, v, bias):
    z = jnp.einsum('qhd,khd->hqk', qb, k).astype(jnp.float32) * (SB_HEAD_DIM ** -0.5)
    z = z + bias.astype(jnp.float32)[:, None, None]
    tq = q_start + jnp.arange(qb.shape[0])
    mask = jnp.arange(k.shape[0])[None, :] < tq[:, None]
    log_keep = jnp.where(mask, jax.nn.log_sigmoid(-z), 0.0)
    log_w = jax.nn.log_sigmoid(z) + lax.cumsum(log_keep, axis=2, reverse=True) - log_keep
    w = jnp.where(mask, jnp.exp(log_w), 0.0)
    return jnp.einsum('hqk,khd->qhd', w.astype(v.dtype), v)


def sb_attend(q, k, v, q_pos0, bias):
    tq = q.shape[0]
    blk = min(Q_BLOCK, tq)
    nb = -(-tq // blk)
    qp = jnp.pad(q, [(0, nb * blk - tq), (0, 0), (0, 0)]).reshape(nb, blk, N_SB_HEADS, SB_HEAD_DIM)
    starts = q_pos0 + jnp.arange(nb) * blk
    out = lax.map(lambda a: sb_block(a[0], a[1], k, v, bias), (qp, starts))
    return out.reshape(nb * blk, N_SB_HEADS, SB_HEAD_DIM)[:tq]


def peer_block(xb, w_query, sub_keys, expert_u, expert_v):
    t = xb.shape[0]
    q = (xb @ w_query).reshape(t, PEER_HEADS, 2, PEER_QDIM // 2)
    s = jnp.einsum('thcd,hckd->thck', q, sub_keys).astype(jnp.float32)
    sv, si = lax.top_k(s, PEER_TOPK)
    cand = sv[:, :, 0, :, None] + sv[:, :, 1, None, :]
    cidx = si[:, :, 0, :, None] * PEER_KEYS + si[:, :, 1, None, :]
    fv, fi = lax.top_k(cand.reshape(t, PEER_HEADS, -1), PEER_TOPK)
    eidx = jnp.take_along_axis(cidx.reshape(t, PEER_HEADS, -1), fi, axis=-1)
    g = jax.nn.softmax(fv, axis=-1)
    u = expert_u[eidx]
    hdn = jax.nn.gelu(jnp.einsum('thkd,td->thk', u, xb), approximate=False)
    return jnp.einsum('thk,thkd->td', (g * hdn).astype(xb.dtype), expert_v[eidx])


def peer_ffn(x2d, w_query, sub_keys, expert_u, expert_v):
    n = x2d.shape[0]
    nb = -(-n // PEER_TOK_BLOCK)
    xp = jnp.pad(x2d, [(0, nb * PEER_TOK_BLOCK - n), (0, 0)]).reshape(nb, PEER_TOK_BLOCK, D_MODEL)
    out = lax.map(lambda xb: peer_block(xb, w_query, sub_keys, expert_u, expert_v), xp)
    return out.reshape(nb * PEER_TOK_BLOCK, D_MODEL)[:n]


def trunk_layer(x, attend, conv_buf, ssm_state, norm_mix, w_in, conv_w, conv_b, dt_bias, a_log,
                d_skip, ssm_norm_w, sb_bias, w_branch_a, w_branch_b, w_out, norm_ffn, w_query,
                sub_keys, expert_u, expert_v):
    b, t, _ = x.shape
    xn = rmsnorm(x, norm_mix)
    proj = xn @ w_in
    splits = [int(c) for c in np.cumsum(IN_SIZES)[:-1]]
    z, xbc, dt_raw, q, k, v, ga, gb = jnp.split(proj, splits, axis=-1)
    ya, new_conv, new_ssm = mamba_branch(z, xbc, dt_raw, conv_buf, ssm_state, conv_w, conv_b,
                                         dt_bias, a_log, d_skip, ssm_norm_w)
    q = q.reshape(b, t, N_SB_HEADS, SB_HEAD_DIM)
    k = k.reshape(b, t, N_SB_HEADS, SB_HEAD_DIM)
    v = v.reshape(b, t, N_SB_HEADS, SB_HEAD_DIM)
    yb = attend(q, k, v, sb_bias).reshape(b, t, SB_WIDTH)
    merged = jax.nn.sigmoid(ga) * (ya @ w_branch_a) + jax.nn.sigmoid(gb) * (yb @ w_branch_b)
    h = x + merged @ w_out
    hn = rmsnorm(h, norm_ffn).reshape(b * t, D_MODEL)
    h = h + peer_ffn(hn, w_query, sub_keys, expert_u, expert_v).reshape(b, t, D_MODEL)
    return h, k, v, new_ssm, new_conv


def setup_inputs(seed: int = 0) -> dict:
    key = jax.random.key(seed)
    ks = jax.random.split(key, 26)
    nrm = lambda i, shape, scale: jax.random.normal(ks[i], shape, jnp.float32) * scale
    n_pages = PAST_LEN // PAGE_SIZE
    n_used = DEC_BATCH * n_pages
    n_phys = n_used + (n_used + 3) // 4
    page_table = jax.random.permutation(ks[0], n_phys)[:n_used].reshape(DEC_BATCH, n_pages).astype(jnp.int32)
    dt0 = jnp.exp(jax.random.uniform(ks[10], (DEPTH, N_SSM_HEADS), jnp.float32,
                                     minval=math.log(1e-3), maxval=math.log(1e-1)))
    spans_log2 = jnp.linspace(SB_MIN_LOG2_SPAN, SB_MAX_LOG2_SPAN, N_SB_HEADS, dtype=jnp.float32)
    sb_bias = -spans_log2 * math.log(2.0) + nrm(24, (DEPTH, N_SB_HEADS), 0.01)
    return {
        'x_prompt': nrm(1, (BATCH, SEQ, D_MODEL), 1.0),
        'x_sample': nrm(2, (DEC_BATCH, DEC_SEQ, D_MODEL), 1.0),
        'cache_k': nrm(3, (DEPTH, n_phys, PAGE_SIZE, N_SB_HEADS, SB_HEAD_DIM), 1.0),
        'cache_v': nrm(4, (DEPTH, n_phys, PAGE_SIZE, N_SB_HEADS, SB_HEAD_DIM), 1.0),
        'state_ssm': nrm(5, (DEPTH, DEC_BATCH, N_SSM_HEADS, SSM_HEAD_DIM, D_STATE), 0.5),
        'state_conv': nrm(6, (DEPTH, DEC_BATCH, CONV_W - 1, CONV_DIM), 1.0),
        'page_table': page_table,
        'norm_mix': 1.0 + nrm(7, (DEPTH, D_MODEL), 0.01),
        'w_in': nrm(8, (DEPTH, D_MODEL, IN_COLS), D_MODEL ** -0.5),
        'conv_w': nrm(9, (DEPTH, CONV_W, CONV_DIM), CONV_W ** -0.5),
        'conv_b': nrm(11, (DEPTH, CONV_DIM), 0.01),
        'dt_bias': dt0 + jnp.log(-jnp.expm1(-dt0)),
        'a_log': jnp.log(jax.random.uniform(ks[12], (DEPTH, N_SSM_HEADS), jnp.float32, minval=1.0, maxval=16.0)),
        'd_skip': 1.0 + nrm(13, (DEPTH, N_SSM_HEADS), 0.01),
        'ssm_norm_w': 1.0 + nrm(14, (DEPTH, D_INNER), 0.01),
        'sb_bias': sb_bias,
        'w_branch_a': nrm(15, (DEPTH, D_INNER, D_MODEL), D_INNER ** -0.5),
        'w_branch_b': nrm(16, (DEPTH, SB_WIDTH, D_MODEL), SB_WIDTH ** -0.5),
        'w_out': nrm(17, (DEPTH, D_MODEL, D_MODEL), D_MODEL ** -0.5),
        'norm_ffn': 1.0 + nrm(18, (DEPTH, D_MODEL), 0.01),
        'w_query': nrm(19, (DEPTH, D_MODEL, PEER_HEADS * PEER_QDIM), D_MODEL ** -0.5),
        'sub_keys': nrm(20, (DEPTH, PEER_HEADS, 2, PEER_KEYS, PEER_QDIM // 2), (PEER_QDIM // 2) ** -0.5),
        'expert_u': nrm(21, (DEPTH, PEER_EXPERTS, D_MODEL), D_MODEL ** -0.5),
        'expert_v': nrm(22, (DEPTH, PEER_EXPERTS, D_MODEL), 0.1),
        'norm_final': 1.0 + nrm(23, (D_MODEL,), 0.01),
    }


def reference(x_prompt, x_sample, cache_k, cache_v, state_ssm, state_conv, page_table,
              norm_mix, w_in, conv_w, conv_b, dt_bias, a_log, d_skip, ssm_norm_w, sb_bias,
              w_branch_a, w_branch_b, w_out, norm_ffn, w_query, sub_keys, expert_u, expert_v,
              norm_final):
    hp, hs = x_prompt, x_sample
    kp_l, vp_l, sp_l, cp_l, ks_l, vs_l, ss_l, cs_l = [], [], [], [], [], [], [], []
    for l in range(DEPTH):
        lw = (norm_mix[l], w_in[l], conv_w[l], conv_b[l], dt_bias[l], a_log[l], d_skip[l],
              ssm_norm_w[l], sb_bias[l], w_branch_a[l], w_branch_b[l], w_out[l], norm_ffn[l],
              w_query[l], sub_keys[l], expert_u[l], expert_v[l])

        def prompt_attend(q, k, v, bias):
            return lax.map(lambda a: sb_attend(a[0], a[1], a[2], 0, bias), (q, k, v))

        ck, cv = cache_k[l], cache_v[l]

        def sample_attend(q, k, v, bias, ck=ck, cv=cv):
            def one(a):
                qb, kb, vb, pt = a
                k_all = jnp.concatenate([ck[pt].reshape(-1, N_SB_HEADS, SB_HEAD_DIM), kb], axis=0)
                v_all = jnp.concatenate([cv[pt].reshape(-1, N_SB_HEADS, SB_HEAD_DIM), vb], axis=0)
                return sb_attend(qb, k_all, v_all, PAST_LEN, bias)
            return lax.map(one, (q, k, v, page_table))

        conv0 = jnp.zeros((BATCH, CONV_W - 1, CONV_DIM), x_prompt.dtype)
        ssm0 = jnp.zeros((BATCH, N_SSM_HEADS, SSM_HEAD_DIM, D_STATE), jnp.float32)
        hp, kp, vp, sp, cp = trunk_layer(hp, prompt_attend, conv0, ssm0, *lw)
        hs, kq, vq, sq, cq = trunk_layer(hs, sample_attend, state_conv[l], state_ssm[l], *lw)
        kp_l.append(kp); vp_l.append(vp); sp_l.append(sp); cp_l.append(cp)
        ks_l.append(kq); vs_l.append(vq); ss_l.append(sq); cs_l.append(cq)
    y_prompt = rmsnorm(hp, norm_final)
    y_sample = rmsnorm(hs, norm_final)
    return (y_prompt, y_sample,
            jnp.stack(kp_l), jnp.stack(vp_l), jnp.stack(sp_l), jnp.stack(cp_l),
            jnp.stack(ks_l), jnp.stack(vs_l), jnp.stack(ss_l), jnp.stack(cs_l))
```

```python
import functools
import math

import numpy as np
import jax
import jax.numpy as jnp
from jax import lax
from jax.experimental import pallas as pl
from jax.experimental.pallas import tpu as pltpu

F32 = jnp.float32
BF16 = jnp.bfloat16
EPS = 1e-6
NEG_BIG = -1e30

HEAD_P = 64
GROUP_HEADS = 8
GROUP_W = HEAD_P * GROUP_HEADS
D_STATE = 128
SB_DH = 128
PEER_K = 16
PEER_KEYS = 128


def _cparams(sem, vmem_mb):
    return pltpu.CompilerParams(dimension_semantics=sem,
                                vmem_limit_bytes=vmem_mb * 1024 * 1024)


def _softplus(x):
    return jnp.maximum(x, 0.0) + jnp.log1p(jnp.exp(-jnp.abs(x)))


def _silu(x):
    return x * jax.nn.sigmoid(x)


def _split2(x):
    hi = x.astype(BF16)
    lo = (x - hi.astype(F32)).astype(BF16)
    return hi, lo


def _split3(x):
    hi = x.astype(BF16)
    r = x - hi.astype(F32)
    mid = r.astype(BF16)
    lo = (r - mid.astype(F32)).astype(BF16)
    return hi, mid, lo


def _dot(a, b):
    return jnp.dot(a, b, preferred_element_type=F32)


def _dot_nt(a, b):
    return lax.dot_general(a, b, (((1,), (1,)), ((), ())), preferred_element_type=F32)


def _dot_tn(a, b):
    return lax.dot_general(a, b, (((0,), (0,)), ((), ())), preferred_element_type=F32)


def _rms_kernel(x_ref, w_ref, o_ref):
    x = x_ref[...]
    r = x * lax.rsqrt(jnp.mean(x * x, axis=-1, keepdims=True) + EPS)
    o_ref[...] = (r * w_ref[...]).astype(o_ref.dtype)


def _rms_add_kernel(x_ref, y_ref, w_ref, o_ref):
    x = x_ref[...] + y_ref[...]
    r = x * lax.rsqrt(jnp.mean(x * x, axis=-1, keepdims=True) + EPS)
    o_ref[...] = (r * w_ref[...]).astype(o_ref.dtype)


def rmsnorm(x, w, out_dtype, tm=128, add=None):
    m, d = x.shape
    row = pl.BlockSpec((tm, d), lambda i: (i, 0))
    wspec = pl.BlockSpec((1, d), lambda i: (0, 0))
    if add is None:
        kern, ins, specs = _rms_kernel, (x, w.reshape(1, d)), [row, wspec]
    else:
        kern, ins, specs = _rms_add_kernel, (x, add, w.reshape(1, d)), [row, row, wspec]
    return pl.pallas_call(
        kern, grid=(m // tm,), in_specs=specs, out_specs=row,
        out_shape=jax.ShapeDtypeStruct((m, d), out_dtype),
        compiler_params=_cparams(("parallel",), 40), name="rmsnorm")(*ins)


def _mm_kernel(a_ref, w_ref, o_ref):
    o_ref[...] = _dot(a_ref[...], w_ref[...]).astype(o_ref.dtype)


def matmul(a, w, out_dtype, tm, tn):
    m, k = a.shape
    n = w.shape[1]
    return pl.pallas_call(
        _mm_kernel, grid=(m // tm, n // tn),
        in_specs=[pl.BlockSpec((tm, k), lambda i, j: (i, 0)),
                  pl.BlockSpec((k, tn), lambda i, j: (0, j))],
        out_specs=pl.BlockSpec((tm, tn), lambda i, j: (i, j)),
        out_shape=jax.ShapeDtypeStruct((m, n), out_dtype),
        compiler_params=_cparams(("parallel", "arbitrary"), 52), name="matmul")(a, w)


def _mm_resid_kernel(a_ref, w_ref, x_ref, o_ref):
    o_ref[...] = x_ref[...] + _dot(a_ref[...], w_ref[...])


def matmul_resid(a, w, x, tm, tn):
    m, k = a.shape
    n = w.shape[1]
    return pl.pallas_call(
        _mm_resid_kernel, grid=(m // tm, n // tn),
        in_specs=[pl.BlockSpec((tm, k), lambda i, j: (i, 0)),
                  pl.BlockSpec((k, tn), lambda i, j: (0, j)),
                  pl.BlockSpec((tm, tn), lambda i, j: (i, j))],
        out_specs=pl.BlockSpec((tm, tn), lambda i, j: (i, j)),
        out_shape=jax.ShapeDtypeStruct((m, n), F32),
        compiler_params=_cparams(("parallel", "arbitrary"), 52), name="matmul_resid")(a, w, x)


def _mm_merge_kernel(ya_ref, yb_ref, wa_ref, wb_ref, ga_ref, gb_ref, o_ref):
    a = _dot(ya_ref[...], wa_ref[...])
    b = _dot(yb_ref[...], wb_ref[...])
    o_ref[...] = (jax.nn.sigmoid(ga_ref[...]) * a + jax.nn.sigmoid(gb_ref[...]) * b).astype(o_ref.dtype)


def merge_branches(ya, yb, wa, wb, proj, ga_col, gb_col, tm, tn):
    m, ka = ya.shape
    kb = yb.shape[1]
    n = wa.shape[1]
    ga0, gb0 = ga_col // tn, gb_col // tn
    return pl.pallas_call(
        _mm_merge_kernel, grid=(m // tm, n // tn),
        in_specs=[pl.BlockSpec((tm, ka), lambda i, j: (i, 0)),
                  pl.BlockSpec((tm, kb), lambda i, j: (i, 0)),
                  pl.BlockSpec((ka, tn), lambda i, j: (0, j)),
                  pl.BlockSpec((kb, tn), lambda i, j: (0, j)),
                  pl.BlockSpec((tm, tn), lambda i, j: (i, ga0 + j)),
                  pl.BlockSpec((tm, tn), lambda i, j: (i, gb0 + j))],
        out_specs=pl.BlockSpec((tm, tn), lambda i, j: (i, j)),
        out_shape=jax.ShapeDtypeStruct((m, n), BF16),
        compiler_params=_cparams(("parallel", "arbitrary"), 52), name="merge")(
            ya, yb, wa, wb, proj, proj)


def _conv_silu(ext, w_ref, b_ref, L):
    acc = b_ref[...] + w_ref[3:4, :] * ext[8:8 + L, :]
    acc = acc + w_ref[2:3, :] * ext[7:7 + L, :]
    acc = acc + w_ref[1:2, :] * ext[6:6 + L, :]
    acc = acc + w_ref[0:1, :] * ext[5:5 + L, :]
    return _silu(acc)


def _pair_terms(pr, x, dt, acs, acs_t, last_t, cb, ch, mask, hpn, lo_half):
    ra, rb = 2 * pr, 2 * pr + 1
    cola, colb = acs[:, ra:ra + 1], acs[:, rb:rb + 1]
    rowa, rowb = acs_t[ra:ra + 1, :], acs_t[rb:rb + 1, :]
    ma = cb * jnp.exp(jnp.where(mask, cola - rowa, NEG_BIG))
    mb = cb * jnp.exp(jnp.where(mask, colb - rowb, NEG_BIG))
    mcat = jnp.concatenate([ma.astype(BF16), mb.astype(BF16)], axis=1)
    xp = x[:, pr * 128:(pr + 1) * 128]
    xdt = xp * jnp.where(lo_half, dt[:, ra:ra + 1], dt[:, rb:rb + 1])
    rhs = jnp.concatenate([jnp.where(lo_half, xdt, 0.0), jnp.where(lo_half, 0.0, xdt)],
                          axis=0).astype(BF16)
    yd = _dot(mcat, rhs)
    eap = jnp.where(lo_half, jnp.exp(cola), jnp.exp(colb))
    dsk = jnp.where(lo_half[0:1, :], hpn[2:3, ra:ra + 1], hpn[2:3, rb:rb + 1])
    y = yd + ch[:, pr * 128:(pr + 1) * 128] * eap + xp * dsk
    return y, xdt, eap


def _ssd_prompt_kernel(xs_ref, b_ref, c_ref, z_ref, dtn_ref, dtt_ref,
                       cwx_ref, cwb_ref, cwc_ref, cbx_ref, cbb_ref, cbc_ref,
                       hpn_ref, hpt_ref, nw_ref, lin_ref, uin_ref,
                       y_ref, hout_ref, extx, extb, extc, hs, *, L):
    c = pl.program_id(1)

    @pl.when(c == 0)
    def _():
        extx[0:8, :] = jnp.zeros((8, GROUP_W), F32)
        extb[0:8, :] = jnp.zeros((8, D_STATE), F32)
        extc[0:8, :] = jnp.zeros((8, D_STATE), F32)
        hs[...] = jnp.zeros_like(hs)

    extx[8:8 + L, :] = xs_ref[...]
    extb[8:8 + L, :] = b_ref[...]
    extc[8:8 + L, :] = c_ref[...]
    x = _conv_silu(extx, cwx_ref, cbx_ref, L)
    bm = _conv_silu(extb, cwb_ref, cbb_ref, L).astype(BF16)
    cm = _conv_silu(extc, cwc_ref, cbc_ref, L).astype(BF16)
    extx[0:8, :] = extx[L:L + 8, :]
    extb[0:8, :] = extb[L:L + 8, :]
    extc[0:8, :] = extc[L:L + 8, :]

    hpn = hpn_ref[0]
    hpt = hpt_ref[0]
    dt = _softplus(dtn_ref[0] + hpn[0:1, :])
    d_a = dt * (-jnp.exp(hpn[1:2, :]))
    d_at = _softplus(dtt_ref[0] + hpt[:, 0:1]) * (-jnp.exp(hpt[:, 1:2]))
    lin = lin_ref[...]
    uin = uin_ref[...]
    acs = sum(_dot(lin, p) for p in _split3(d_a))
    acs_t = sum(_dot(p, uin) for p in _split3(d_at))

    rows = lax.broadcasted_iota(jnp.int32, (L, L), 0)
    cols = lax.broadcasted_iota(jnp.int32, (L, L), 1)
    causal = rows >= cols
    lo_half = lax.broadcasted_iota(jnp.int32, (L, 128), 1) < HEAD_P
    sub_lo = lax.broadcasted_iota(jnp.int32, (128, D_STATE), 0) < HEAD_P

    cb = _dot_nt(cm, bm)
    ch = _dot_nt(cm, hs[...].astype(BF16))

    ys = []
    for pr in range(GROUP_HEADS // 2):
        ra, rb = 2 * pr, 2 * pr + 1
        y, xdt, _ = _pair_terms(pr, x, dt, acs, acs_t, None, cb, ch, causal, hpn, lo_half)
        ys.append(y)
        lasta = acs_t[ra:ra + 1, L - 1:L]
        lastb = acs_t[rb:rb + 1, L - 1:L]
        dend = jnp.where(lo_half, jnp.exp(lasta - acs[:, ra:ra + 1]),
                         jnp.exp(lastb - acs[:, rb:rb + 1]))
        s_new = _dot_tn((xdt * dend).astype(BF16), bm)
        cd = jnp.where(sub_lo, jnp.exp(lasta), jnp.exp(lastb))
        sl = slice(pr * 128, (pr + 1) * 128)
        hs[sl, :] = hs[sl, :] * cd + s_new

    yg = jnp.concatenate(ys, axis=1)
    u = yg * _silu(z_ref[...])
    u = u * lax.rsqrt(jnp.mean(u * u, axis=-1, keepdims=True) + EPS)
    y_ref[...] = (u * nw_ref[...]).astype(y_ref.dtype)

    @pl.when(c == pl.num_programs(1) - 1)
    def _():
        hout_ref[...] = hs[...]


def _tri_incl(n):
    return np.tril(np.ones((n, n), np.float32))


def ssd_prompt(proj, t, dt_n, dt_t, conv_w, conv_b, hpn, hpt, norm_w, L):
    g = dt_n.shape[0]
    nc = t // L
    d_inner = g * GROUP_W
    xb = d_inner // GROUP_W
    bb = (2 * d_inner) // D_STATE
    cbk = bb + g
    lin = jnp.asarray(_tri_incl(L), BF16)
    uin = jnp.asarray(_tri_incl(L).T, BF16)
    const = lambda gi, ci: (0, 0)
    kern = functools.partial(_ssd_prompt_kernel, L=L)
    return pl.pallas_call(
        kern, grid=(g, nc),
        in_specs=[
            pl.BlockSpec((L, GROUP_W), lambda gi, ci: (ci, xb + gi)),
            pl.BlockSpec((L, D_STATE), lambda gi, ci: (ci, bb + gi)),
            pl.BlockSpec((L, D_STATE), lambda gi, ci: (ci, cbk + gi)),
            pl.BlockSpec((L, GROUP_W), lambda gi, ci: (ci, gi)),
            pl.BlockSpec((1, L, 128), lambda gi, ci: (gi, ci, 0)),
            pl.BlockSpec((1, 16, L), lambda gi, ci: (gi, 0, ci)),
            pl.BlockSpec((4, GROUP_W), lambda gi, ci: (0, gi)),
            pl.BlockSpec((4, D_STATE), lambda gi, ci: (0, bb - xb * 4 + gi)),
            pl.BlockSpec((4, D_STATE), lambda gi, ci: (0, bb - xb * 4 + g + gi)),
            pl.BlockSpec((1, GROUP_W), lambda gi, ci: (0, gi)),
            pl.BlockSpec((1, D_STATE), lambda gi, ci: (0, bb - xb * 4 + gi)),
            pl.BlockSpec((1, D_STATE), lambda gi, ci: (0, bb - xb * 4 + g + gi)),
            pl.BlockSpec((1, 8, 128), lambda gi, ci: (gi, 0, 0)),
            pl.BlockSpec((1, 16, 128), lambda gi, ci: (gi, 0, 0)),
            pl.BlockSpec((1, GROUP_W), lambda gi, ci: (0, gi)),
            pl.BlockSpec((L, L), const),
            pl.BlockSpec((L, L), const),
        ],
        out_specs=[pl.BlockSpec((L, GROUP_W), lambda gi, ci: (ci, gi)),
                   pl.BlockSpec((GROUP_W, D_STATE), lambda gi, ci: (gi, 0))],
        out_shape=[jax.ShapeDtypeStruct((t, d_inner), BF16),
                   jax.ShapeDtypeStruct((d_inner, D_STATE), F32)],
        scratch_shapes=[pltpu.VMEM((L + 8, GROUP_W), F32), pltpu.VMEM((L + 8, D_STATE), F32),
                        pltpu.VMEM((L + 8, D_STATE), F32), pltpu.VMEM((GROUP_W, D_STATE), F32)],
        compiler_params=_cparams(("arbitrary", "arbitrary"), 40), name="ssd_prompt")(
            proj, proj, proj, proj, dt_n, dt_t, conv_w, conv_w, conv_w, conv_b, conv_b, conv_b,
            hpn, hpt, norm_w, lin, uin)


def _ssd_sample_kernel(shx_ref, shb_ref, shc_ref, z_ref, dtn_ref, dtt_ref,
                       cwx_ref, cwb_ref, cwc_ref, cbx_ref, cbb_ref, cbc_ref,
                       hpn_ref, hpt_ref, nw_ref, lseg_ref, useg_ref, sseg_ref, st_ref,
                       y_ref, hout_ref, yacc, cm_s, bm_s, ea_s, xwt_s, cdl_s, *, n_tok, seq):
    b = pl.program_id(1)

    @pl.when(b == 0)
    def _():
        def conv(sh_ref, w_ref, b_ref):
            acc = b_ref[...] + w_ref[0:1, :] * sh_ref[0]
            for k in range(1, 4):
                acc = acc + w_ref[k:k + 1, :] * sh_ref[k]
            return _silu(acc)

        x = conv(shx_ref, cwx_ref, cbx_ref)
        bm = conv(shb_ref, cwb_ref, cbb_ref).astype(BF16)
        cm = conv(shc_ref, cwc_ref, cbc_ref).astype(BF16)
        hpn = hpn_ref[0]
        hpt = hpt_ref[0]
        dt = _softplus(dtn_ref[0] + hpn[0:1, :])
        d_a = dt * (-jnp.exp(hpn[1:2, :]))
        d_at = _softplus(dtt_ref[0] + hpt[:, 0:1]) * (-jnp.exp(hpt[:, 1:2]))
        lseg = lseg_ref[...]
        useg = useg_ref[...]
        sseg = sseg_ref[...]
        pa, pat = _split3(d_a), _split3(d_at)
        acs = sum(_dot(lseg, p) for p in pa)
        tot = sum(_dot(sseg, p) for p in pa)
        acs_t = sum(_dot(p, useg) for p in pat)
        tot_t = sum(_dot(p, sseg) for p in pat)
        mask = lseg > 0
        lo_half = lax.broadcasted_iota(jnp.int32, (n_tok, 128), 1) < HEAD_P
        cb = _dot_nt(cm, bm)
        zero_ch = jnp.zeros((n_tok, GROUP_W), F32)
        ys, xws, eas = [], [], []
        for pr in range(GROUP_HEADS // 2):
            ra, rb = 2 * pr, 2 * pr + 1
            y, xdt, eap = _pair_terms(pr, x, dt, acs, acs_t, None, cb, zero_ch, mask, hpn, lo_half)
            dend = jnp.where(lo_half, jnp.exp(tot[:, ra:ra + 1] - acs[:, ra:ra + 1]),
                             jnp.exp(tot[:, rb:rb + 1] - acs[:, rb:rb + 1]))
            ys.append(y)
            xws.append(xdt * dend)
            eas.append(eap)
        yacc[...] = jnp.concatenate(ys, axis=1)
        ea_s[...] = jnp.concatenate(eas, axis=1)
        xwt_s[...] = jnp.transpose(jnp.concatenate(xws, axis=1)).astype(BF16)
        cm_s[...] = cm
        bm_s[...] = bm
        cdl_s[...] = jnp.concatenate(
            [jnp.broadcast_to(tot_t[r:r + 1, :], (HEAD_P, n_tok)) for r in range(GROUP_HEADS)], axis=0)

    h0 = st_ref[0]
    lo = b * seq
    rid = lax.broadcasted_iota(jnp.int32, (n_tok, GROUP_W), 0)
    in_rows = jnp.abs(2 * (rid - lo) - (seq - 1)) < seq
    lid = lax.broadcasted_iota(jnp.int32, (GROUP_W, n_tok), 1)
    in_lanes = jnp.abs(2 * (lid - lo) - (seq - 1)) < seq
    ch = _dot_nt(cm_s[...], h0.astype(BF16))
    yacc[...] += jnp.where(in_rows, ch * ea_s[...], 0.0)
    s_new = _dot(jnp.where(in_lanes, xwt_s[...], jnp.zeros_like(xwt_s[...])), bm_s[...])
    cdcol = jnp.sum(jnp.where(lid == lo, cdl_s[...], 0.0), axis=-1, keepdims=True)
    hout_ref[0] = h0 * jnp.exp(cdcol) + s_new

    @pl.when(b == pl.num_programs(1) - 1)
    def _():
        u = yacc[...] * _silu(z_ref[...])
        u = u * lax.rsqrt(jnp.mean(u * u, axis=-1, keepdims=True) + EPS)
        y_ref[...] = (u * nw_ref[...]).astype(y_ref.dtype)


def ssd_sample(sh, proj, z_row_block, dt_n, dt_t, conv_w, conv_b, hpn, hpt, norm_w, state, seq):
    g = dt_n.shape[0]
    n_tok = sh.shape[1]
    n_seq = n_tok // seq
    d_inner = g * GROUP_W
    xb4 = d_inner // D_STATE
    sid = np.arange(n_tok) // seq
    same = (sid[:, None] == sid[None, :]).astype(np.float32)
    lseg = jnp.asarray(same * _tri_incl(n_tok), BF16)
    useg = jnp.asarray((same * _tri_incl(n_tok)).T, BF16)
    sseg = jnp.asarray(same, BF16)
    const = lambda gi, bi: (0, 0)
    kern = functools.partial(_ssd_sample_kernel, n_tok=n_tok, seq=seq)
    return pl.pallas_call(
        kern, grid=(g, n_seq),
        in_specs=[
            pl.BlockSpec((4, n_tok, GROUP_W), lambda gi, bi: (0, 0, gi)),
            pl.BlockSpec((4, n_tok, D_STATE), lambda gi, bi: (0, 0, xb4 + gi)),
            pl.BlockSpec((4, n_tok, D_STATE), lambda gi, bi: (0, 0, xb4 + g + gi)),
            pl.BlockSpec((n_tok, GROUP_W), lambda gi, bi: (z_row_block, gi)),
            pl.BlockSpec((1, n_tok, 128), lambda gi, bi: (gi, 0, 0)),
            pl.BlockSpec((1, 16, n_tok), lambda gi, bi: (gi, 0, 0)),
            pl.BlockSpec((4, GROUP_W), lambda gi, bi: (0, gi)),
            pl.BlockSpec((4, D_STATE), lambda gi, bi: (0, xb4 + gi)),
            pl.BlockSpec((4, D_STATE), lambda gi, bi: (0, xb4 + g + gi)),
            pl.BlockSpec((1, GROUP_W), lambda gi, bi: (0, gi)),
            pl.BlockSpec((1, D_STATE), lambda gi, bi: (0, xb4 + gi)),
            pl.BlockSpec((1, D_STATE), lambda gi, bi: (0, xb4 + g + gi)),
            pl.BlockSpec((1, 8, 128), lambda gi, bi: (gi, 0, 0)),
            pl.BlockSpec((1, 16, 128), lambda gi, bi: (gi, 0, 0)),
            pl.BlockSpec((1, GROUP_W), lambda gi, bi: (0, gi)),
            pl.BlockSpec((n_tok, n_tok), const),
            pl.BlockSpec((n_tok, n_tok), const),
            pl.BlockSpec((n_tok, n_tok), const),
            pl.BlockSpec((1, GROUP_W, D_STATE), lambda gi, bi: (bi, gi, 0)),
        ],
        out_specs=[pl.BlockSpec((n_tok, GROUP_W), lambda gi, bi: (0, gi)),
                   pl.BlockSpec((1, GROUP_W, D_STATE), lambda gi, bi: (bi, gi, 0))],
        out_shape=[jax.ShapeDtypeStruct((n_tok, d_inner), BF16),
                   jax.ShapeDtypeStruct(state.shape, F32)],
        scratch_shapes=[pltpu.VMEM((n_tok, GROUP_W), F32), pltpu.VMEM((n_tok, D_STATE), BF16),
                        pltpu.VMEM((n_tok, D_STATE), BF16), pltpu.VMEM((n_tok, GROUP_W), F32),
                        pltpu.VMEM((GROUP_W, n_tok), BF16), pltpu.VMEM((GROUP_W, n_tok), F32)],
        compiler_params=_cparams(("arbitrary", "arbitrary"), 40), name="ssd_sample")(
            sh, sh, sh, proj, dt_n, dt_t, conv_w, conv_w, conv_w, conv_b, conv_b, conv_b,
            hpn, hpt, norm_w, lseg, useg, sseg, state)


def _sb_prompt_kernel(bias_ref, q_ref, k_ref, v_ref, u_ref, o_ref, kbf, vbf, *, blk, scale):
    h = pl.program_id(0)
    i = pl.program_id(1)

    @pl.when(i == 0)
    def _():
        kbf[...] = k_ref[...].astype(BF16)
        vbf[...] = v_ref[...].astype(BF16)

    q = q_ref[...].astype(BF16)
    bias = bias_ref[h]
    umat = u_ref[...]
    rows = lax.broadcasted_iota(jnp.int32, (blk, blk), 0)
    cols = lax.broadcasted_iota(jnp.int32, (blk, blk), 1)
    strict = cols < rows

    def block(j, carry, masked):
        acc, run = carry
        start = pl.multiple_of(j * blk, blk)
        kb = kbf[pl.ds(start, blk), :]
        vb = vbf[pl.ds(start, blk), :]
        z = _dot_nt(q, kb) * scale + bias
        sp = _softplus(z)
        lk = -sp
        if masked:
            lk = jnp.where(strict, lk, 0.0)
        hi, lo = _split2(lk)
        suffix = _dot(hi, umat) + _dot(lo, umat)
        w = jnp.exp((z - sp) + suffix + run)
        if masked:
            w = jnp.where(strict, w, 0.0)
        acc = acc + _dot(w.astype(BF16), vb)
        run = run + jnp.sum(lk, axis=-1, keepdims=True)
        return acc, run

    carry = (jnp.zeros((blk, SB_DH), F32), jnp.zeros((blk, 1), F32))
    carry = block(i, carry, True)
    carry = lax.fori_loop(0, i, lambda t, c: block(i - 1 - t, c, False), carry)
    o_ref[...] = carry[0].astype(o_ref.dtype)


def _tri_later(n):
    return np.tril(np.ones((n, n), np.float32), -1)


def sb_prompt(proj, t, n_heads, q_col, k_col, v_col, bias, blk):
    qb, kb, vb = q_col // SB_DH, k_col // SB_DH, v_col // SB_DH
    umat = jnp.asarray(_tri_later(blk), BF16)
    kern = functools.partial(_sb_prompt_kernel, blk=blk, scale=SB_DH ** -0.5)
    return pl.pallas_call(
        kern, grid=(n_heads, t // blk),
        in_specs=[
            pl.BlockSpec(memory_space=pltpu.SMEM),
            pl.BlockSpec((blk, SB_DH), lambda h, i: (i, qb + h)),
            pl.BlockSpec((t, SB_DH), lambda h, i: (0, kb + h)),
            pl.BlockSpec((t, SB_DH), lambda h, i: (0, vb + h)),
            pl.BlockSpec((blk, blk), lambda h, i: (0, 0)),
        ],
        out_specs=pl.BlockSpec((blk, SB_DH), lambda h, i: (i, h)),
        out_shape=jax.ShapeDtypeStruct((t, n_heads * SB_DH), BF16),
        scratch_shapes=[pltpu.VMEM((t, SB_DH), BF16), pltpu.VMEM((t, SB_DH), BF16)],
        compiler_params=_cparams(("arbitrary", "arbitrary"), 48), name="sb_prompt")(
            bias, proj, proj, proj, umat)


def _sb_sample_kernel(pt_ref, qbd_ref, brow_ref, kn_ref, vn_ref, tn_ref, tp_ref,
                      k0, k1, k2, k3, v0, v1, v2, v3,
                      o_ref, acc, run, knf, vnf, *, scale, n_heads, seq, page):
    s = pl.program_id(1)
    qbd = qbd_ref[0]
    brow = brow_ref[...]
    nq = n_heads * 8

    def process(kf, vf, tmat, mask):
        z = _dot(kf.astype(BF16), qbd) * scale + brow
        sp = _softplus(z)
        lk = -sp
        if mask is not None:
            lk = jnp.where(mask, lk, 0.0)
        hi, lo = _split2(lk)
        suffix = _dot(tmat, hi) + _dot(tmat, lo)
        w = jnp.exp((z - sp) + suffix + run[...])
        if mask is not None:
            w = jnp.where(mask, w, 0.0)
        acc[...] += _dot_tn(w.astype(BF16), vf.astype(BF16))
        run[...] += jnp.sum(lk, axis=0, keepdims=True)

    @pl.when(s == 0)
    def _():
        acc[...] = jnp.zeros_like(acc)
        run[...] = jnp.zeros_like(run)
        knf[...] = jnp.zeros_like(knf)
        vnf[...] = jnp.zeros_like(vnf)
        knf[0:8, :] = kn_ref[0]
        vnf[0:8, :] = vn_ref[0]
        key = lax.broadcasted_iota(jnp.int32, (page, nq), 0)
        qpos = lax.broadcasted_iota(jnp.int32, (page, nq), 1) % 8
        process(knf[...], vnf[...], tn_ref[...], key < qpos)

    kcat = jnp.concatenate([k3[0], k2[0], k1[0], k0[0]], axis=0)
    vcat = jnp.concatenate([v3[0], v2[0], v1[0], v0[0]], axis=0)
    process(kcat, vcat, tp_ref[...], None)

    @pl.when(s == pl.num_programs(1) - 1)
    def _():
        for h in range(n_heads):
            o_ref[0, :, h * SB_DH:(h + 1) * SB_DH] = (
                acc[h * 8:h * 8 + seq, h * SB_DH:(h + 1) * SB_DH].astype(o_ref.dtype))


def sb_sample(q_s, k_s, v_s, cache_k, cache_v, page_table, bias, n_heads, seq):
    n_seq, n_pages = page_table.shape
    page = cache_k.shape[1]
    width = n_heads * SB_DH
    nq = n_heads * 8
    ppb = 4
    steps = n_pages // ppb
    q4 = jnp.pad(q_s.reshape(n_seq, seq, n_heads, SB_DH), ((0, 0), (0, 8 - seq), (0, 0), (0, 0)))
    eye = jnp.eye(n_heads, dtype=F32)
    qbd = jnp.einsum("bthd,hg->bhdgt", q4, eye).reshape(n_seq, width, nq).astype(BF16)
    brow = jnp.repeat(bias, 8).reshape(1, nq)
    pad = lambda a: jnp.pad(a.reshape(n_seq, seq, width), ((0, 0), (0, 8 - seq), (0, 0)))
    tn = jnp.asarray(_tri_later(page).T, BF16)
    tp = jnp.asarray(_tri_later(ppb * page).T, BF16)

    def page_spec(u):
        return pl.BlockSpec((1, page, width),
                            lambda b, s, pt: (pt[b, n_pages - 1 - (ppb * s + u)], 0, 0))

    kern = functools.partial(_sb_sample_kernel, scale=SB_DH ** -0.5, n_heads=n_heads, seq=seq,
                             page=page)
    grid_spec = pltpu.PrefetchScalarGridSpec(
        num_scalar_prefetch=1, grid=(n_seq, steps),
        in_specs=[
            pl.BlockSpec((1, width, nq), lambda b, s, pt: (b, 0, 0)),
            pl.BlockSpec((1, nq), lambda b, s, pt: (0, 0)),
            pl.BlockSpec((1, 8, width), lambda b, s, pt: (b, 0, 0)),
            pl.BlockSpec((1, 8, width), lambda b, s, pt: (b, 0, 0)),
            pl.BlockSpec((page, page), lambda b, s, pt: (0, 0)),
            pl.BlockSpec((ppb * page, ppb * page), lambda b, s, pt: (0, 0)),
        ] + [page_spec(u) for u in range(ppb)] + [page_spec(u) for u in range(ppb)],
        out_specs=pl.BlockSpec((1, seq, width), lambda b, s, pt: (b, 0, 0)),
        scratch_shapes=[pltpu.VMEM((nq, width), F32), pltpu.VMEM((1, nq), F32),
                        pltpu.VMEM((page, width), F32), pltpu.VMEM((page, width), F32)])
    return pl.pallas_call(
        kern, grid_spec=grid_spec,
        out_shape=jax.ShapeDtypeStruct((n_seq, seq, width), BF16),
        compiler_params=_cparams(("arbitrary", "arbitrary"), 48), name="sb_sample")(
            page_table, qbd, brow, pad(k_s), pad(v_s), tn, tp,
            *([cache_k] * ppb), *([cache_v] * ppb))


def _top16_rows(s):
    n = s.shape[0]
    rid = lax.broadcasted_iota(jnp.int32, s.shape, 0)
    out = []
    for _ in range(PEER_K):
        m = jnp.max(s, axis=0, keepdims=True)
        first = jnp.min(jnp.where(s == m, rid, n), axis=0, keepdims=True)
        out.append(m)
        s = jnp.where(rid == first, -jnp.inf, s)
    return out


def _peer_topk_kernel(q_ref, keys_ref, s_ref, st_ref, *, n_heads):
    tm = q_ref.shape[0]
    for h in range(n_heads):
        tops = []
        for c in range(2):
            qh = q_ref[:, (2 * h + c) * 128:(2 * h + c + 1) * 128]
            sc = _dot_nt(keys_ref[h, c].astype(BF16), qh)
            s_ref[c, h] = sc
            tops.append(_top16_rows(sc))
        sv2 = jnp.concatenate(tops[1], axis=0)
        cand = jnp.concatenate([tops[0][a] + sv2 for a in range(PEER_K)], axis=0)
        best = _top16_rows(cand)
        m = best[0]
        zsum = sum(jnp.exp(v - m) for v in best)
        st_ref[h] = jnp.concatenate(
            [best[PEER_K - 1], m + jnp.log(zsum), jnp.zeros((6, tm), F32)], axis=0)


def peer_topk(qp, sub_keys, tm):
    r = qp.shape[0]
    n_heads = sub_keys.shape[0]
    kern = functools.partial(_peer_topk_kernel, n_heads=n_heads)
    return pl.pallas_call(
        kern, grid=(r // tm,),
        in_specs=[pl.BlockSpec((tm, qp.shape[1]), lambda i: (i, 0)),
                  pl.BlockSpec(sub_keys.shape, lambda i: (0, 0, 0, 0))],
        out_specs=[pl.BlockSpec((2, n_heads, PEER_KEYS, tm), lambda i: (0, 0, 0, i)),
                   pl.BlockSpec((n_heads, 8, tm), lambda i: (0, 0, i))],
        out_shape=[jax.ShapeDtypeStruct((2, n_heads, PEER_KEYS, r), F32),
                   jax.ShapeDtypeStruct((n_heads, 8, r), F32)],
        compiler_params=_cparams(("parallel",), 40), name="peer_topk")(qp, sub_keys)


def _peer_dense_kernel(hn_ref, u_ref, v_ref, s1_ref, s2_ref, st_ref, o_ref, *, n_heads, te):
    e = pl.program_id(1)

    @pl.when(e == 0)
    def _():
        o_ref[...] = jnp.zeros_like(o_ref)

    hid = _dot_nt(u_ref[...], hn_ref[...])
    act = 0.5 * hid * (1.0 + lax.erf(hid * (2.0 ** -0.5)))
    parts = []
    for ii in range(te // PEER_KEYS):
        wsum = None
        for h in range(n_heads):
            sc = s1_ref[ii, h:h + 1, :] + s2_ref[h]
            g = jnp.where(sc >= st_ref[h, 0:1, :], jnp.exp(sc - st_ref[h, 1:2, :]), 0.0)
            wsum = g if wsum is None else wsum + g
        parts.append((wsum * act[ii * PEER_KEYS:(ii + 1) * PEER_KEYS, :]).astype(BF16))
    pt = jnp.concatenate(parts, axis=0)
    o_ref[...] += _dot_tn(pt, v_ref[...])


def peer_dense(hn, eu, ev, s1, s2, st, tm, te):
    r, d = hn.shape
    n_exp = eu.shape[0]
    n_heads = s2.shape[0]
    kern = functools.partial(_peer_dense_kernel, n_heads=n_heads, te=te)
    return pl.pallas_call(
        kern, grid=(r // tm, n_exp // te),
        in_specs=[pl.BlockSpec((tm, d), lambda i, e: (i, 0)),
                  pl.BlockSpec((te, d), lambda i, e: (e, 0)),
                  pl.BlockSpec((te, d), lambda i, e: (e, 0)),
                  pl.BlockSpec((te // PEER_KEYS, n_heads, tm), lambda i, e: (e, 0, i)),
                  pl.BlockSpec((n_heads, PEER_KEYS, tm), lambda i, e: (0, 0, i)),
                  pl.BlockSpec((n_heads, 8, tm), lambda i, e: (0, 0, i))],
        out_specs=pl.BlockSpec((tm, d), lambda i, e: (i, 0)),
        out_shape=jax.ShapeDtypeStruct((r, d), F32),
        compiler_params=_cparams(("parallel", "arbitrary"), 56), name="peer_dense")(
            hn, eu, ev, s1, s2, st)


def kernel(x_prompt, x_sample, cache_k, cache_v, state_ssm, state_conv, page_table, norm_mix, w_in,
           conv_w, conv_b, dt_bias, a_log, d_skip, ssm_norm_w, sb_bias, w_branch_a, w_branch_b,
           w_out, norm_ffn, w_query, sub_keys, expert_u, expert_v, norm_final):
    depth = w_in.shape[0]
    assert depth == 1 and x_prompt.shape[0] == 1
    _, t, d = x_prompt.shape
    n_seq, seq, _ = x_sample.shape
    n_tok = n_seq * seq
    r = t + n_tok
    n_ssm_heads = dt_bias.shape[1]
    d_inner = n_ssm_heads * HEAD_P
    g = n_ssm_heads // GROUP_HEADS
    conv_dim = conv_w.shape[2]
    n_sb = sb_bias.shape[1]
    sbw = n_sb * SB_DH
    assert t % 256 == 0 and n_tok == 128 and r % 640 == 0

    wl = w_in[0]
    dt0 = d_inner + conv_dim
    w_main = jnp.concatenate([wl[:, :dt0], wl[:, dt0 + n_ssm_heads:]], axis=1).astype(BF16)
    w_dt = jnp.pad(wl[:, dt0:dt0 + n_ssm_heads], ((0, 0), (0, 128 - n_ssm_heads))).astype(BF16)
    q_col = dt0
    k_col, v_col = q_col + sbw, q_col + 2 * sbw
    ga_col = q_col + 3 * sbw
    gb_col = ga_col + d

    x_all = jnp.concatenate([x_prompt[0], x_sample.reshape(n_tok, d)], axis=0)
    xn = rmsnorm(x_all, norm_mix[0], BF16)
    proj = matmul(xn, w_main, F32, tm=r // 5, tn=512)
    dt_raw = matmul(xn, w_dt, F32, tm=r // 5, tn=128)[:, :n_ssm_heads]

    def head_layouts(v):
        rows = v.shape[0]
        vg = v.reshape(rows, g, GROUP_HEADS).transpose(1, 0, 2)
        nat = jnp.pad(vg, ((0, 0), (0, 0), (0, 128 - GROUP_HEADS)))
        tr = jnp.pad(vg.transpose(0, 2, 1), ((0, 0), (0, 16 - GROUP_HEADS), (0, 0)))
        return nat, tr

    hp = jnp.stack([dt_bias[0], a_log[0], d_skip[0]], axis=0)
    hp_g = hp.reshape(3, g, GROUP_HEADS).transpose(1, 0, 2)
    hpn = jnp.pad(hp_g, ((0, 0), (0, 5), (0, 128 - GROUP_HEADS)))
    hpt = jnp.pad(hp_g.transpose(0, 2, 1), ((0, 0), (0, 16 - GROUP_HEADS), (0, 125)))
    nw = ssm_norm_w[0].reshape(1, d_inner)
    cw, cbias = conv_w[0], conv_b[0].reshape(1, conv_dim)

    dtn_p, dtt_p = head_layouts(dt_raw[:t])
    ya_p, ssm_p = ssd_prompt(proj, t, dtn_p, dtt_p, cw, cbias, hpn, hpt, nw, L=256)

    xbc_s = proj[t:, d_inner:d_inner + conv_dim].reshape(n_seq, seq, conv_dim)
    ext_s = jnp.concatenate([state_conv[0], xbc_s], axis=1)
    sh = jnp.stack([ext_s[:, k:k + seq].reshape(n_tok, conv_dim) for k in range(4)], axis=0)
    dtn_s, dtt_s = head_layouts(dt_raw[t:])
    ya_s, ssm_s = ssd_sample(sh, proj, t // n_tok, dtn_s, dtt_s, cw, cbias, hpn, hpt, nw,
                             state_ssm[0].reshape(n_seq, d_inner, D_STATE), seq)

    yb_p = sb_prompt(proj, t, n_sb, q_col, k_col, v_col, sb_bias[0], blk=256)
    q_s = proj[t:, q_col:q_col + sbw]
    k_s = proj[t:, k_col:k_col + sbw]
    v_s = proj[t:, v_col:v_col + sbw]
    n_phys, page = cache_k.shape[1], cache_k.shape[2]
    yb_s = sb_sample(q_s, k_s, v_s, cache_k[0].reshape(n_phys, page, sbw),
                     cache_v[0].reshape(n_phys, page, sbw), page_table, sb_bias[0], n_sb, seq)

    ya = jnp.concatenate([ya_p, ya_s], axis=0)
    yb = jnp.concatenate([yb_p, yb_s.reshape(n_tok, sbw)], axis=0)
    merged = merge_branches(ya, yb, w_branch_a[0].astype(BF16), w_branch_b[0].astype(BF16),
                            proj, ga_col, gb_col, tm=640, tn=512)
    h1 = matmul_resid(merged, w_out[0].astype(BF16), x_all, tm=640, tn=512)

    hn = rmsnorm(h1, norm_ffn[0], BF16)
    qp = matmul(hn, w_query[0].astype(BF16), BF16, tm=r // 5, tn=512)
    scores, stats = peer_topk(qp, sub_keys[0], tm=128)
    s1 = scores[0].transpose(1, 0, 2)
    peer = peer_dense(hn, expert_u[0].astype(BF16), expert_v[0].astype(BF16), s1, scores[1], stats,
                      tm=640, te=256)
    y_all = rmsnorm(h1, norm_final, F32, add=peer)

    y_prompt = y_all[:t].reshape(1, t, d)
    y_sample = y_all[t:].reshape(n_seq, seq, d)
    k_prompt = proj[:t, k_col:k_col + sbw].reshape(1, 1, t, n_sb, SB_DH)
    v_prompt = proj[:t, v_col:v_col + sbw].reshape(1, 1, t, n_sb, SB_DH)
    ssm_prompt = ssm_p.reshape(1, 1, n_ssm_heads, HEAD_P, D_STATE)
    conv_prompt = proj[t - 3:t, d_inner:d_inner + conv_dim].reshape(1, 1, 3, conv_dim)
    k_sample = k_s.reshape(1, n_seq, seq, n_sb, SB_DH)
    v_sample = v_s.reshape(1, n_seq, seq, n_sb, SB_DH)
    ssm_sample = ssm_s.reshape(1, n_seq, n_ssm_heads, HEAD_P, D_STATE)
    conv_sample = ext_s[:, seq:].reshape(1, n_seq, 3, conv_dim)
    return (y_prompt, y_sample, k_prompt, v_prompt, ssm_prompt, conv_prompt,
            k_sample, v_sample, ssm_sample, conv_sample)
```

```python
import functools
import math

import numpy as np
import jax
import jax.numpy as jnp
from jax import lax
from jax.experimental import pallas as pl
from jax.experimental.pallas import tpu as pltpu

F32 = jnp.float32
BF16 = jnp.bfloat16
EPS = 1e-6
NEG_BIG = -1e30

HEAD_P = 64
GROUP_HEADS = 8
GROUP_W = HEAD_P * GROUP_HEADS
D_STATE = 128
SB_DH = 128
PEER_K = 16
PEER_KEYS = 128


def _cparams(sem, vmem_mb):
    return pltpu.CompilerParams(dimension_semantics=sem,
                                vmem_limit_bytes=vmem_mb * 1024 * 1024)


def _softplus(x):
    return jnp.maximum(x, 0.0) + jnp.log1p(jnp.exp(-jnp.abs(x)))


LOG2E = 1.4426950408889634


def _softplus2(x):
    return jnp.where(x > 64.0, x, jnp.log(1.0 + jnp.exp2(x)) * LOG2E)


def _silu(x):
    return x * jax.nn.sigmoid(x)


def _split2(x):
    hi = x.astype(BF16)
    lo = (x - hi.astype(F32)).astype(BF16)
    return hi, lo


def _split3(x):
    hi = x.astype(BF16)
    r = x - hi.astype(F32)
    mid = r.astype(BF16)
    lo = (r - mid.astype(F32)).astype(BF16)
    return hi, mid, lo


def _dot(a, b):
    return jnp.dot(a, b, preferred_element_type=F32)


def _dot_nt(a, b):
    return lax.dot_general(a, b, (((1,), (1,)), ((), ())), preferred_element_type=F32)


def _dot_tn(a, b):
    return lax.dot_general(a, b, (((0,), (0,)), ((), ())), preferred_element_type=F32)


def _rms_kernel(x_ref, w_ref, o_ref):
    x = x_ref[...]
    r = x * lax.rsqrt(jnp.mean(x * x, axis=-1, keepdims=True) + EPS)
    o_ref[...] = (r * w_ref[...]).astype(o_ref.dtype)


def _rms_add_kernel(x_ref, y_ref, w_ref, o_ref):
    x = x_ref[...] + y_ref[...]
    r = x * lax.rsqrt(jnp.mean(x * x, axis=-1, keepdims=True) + EPS)
    o_ref[...] = (r * w_ref[...]).astype(o_ref.dtype)


def rmsnorm(x, w, out_dtype, tm=128, add=None):
    m, d = x.shape
    row = pl.BlockSpec((tm, d), lambda i: (i, 0))
    wspec = pl.BlockSpec((1, d), lambda i: (0, 0))
    if add is None:
        kern, ins, specs = _rms_kernel, (x, w.reshape(1, d)), [row, wspec]
    else:
        kern, ins, specs = _rms_add_kernel, (x, add, w.reshape(1, d)), [row, row, wspec]
    return pl.pallas_call(
        kern, grid=(m // tm,), in_specs=specs, out_specs=row,
        out_shape=jax.ShapeDtypeStruct((m, d), out_dtype),
        compiler_params=_cparams(("parallel",), 40), name="rmsnorm")(*ins)


def _mm_kernel(a_ref, w_ref, *o_refs, scale):
    acc = _dot(a_ref[...], w_ref[...])
    if scale is not None:
        acc = acc * scale
    for o_ref in o_refs:
        o_ref[...] = acc.astype(o_ref.dtype)


def matmul(a, w, out_dtypes, tm, tn, col0=0, n=None, scale=None):
    m, k = a.shape
    n = w.shape[1] if n is None else n
    c0 = col0 // tn
    out = pl.BlockSpec((tm, tn), lambda i, j: (i, j))
    return pl.pallas_call(
        functools.partial(_mm_kernel, scale=scale), grid=(m // tm, n // tn),
        in_specs=[pl.BlockSpec((tm, k), lambda i, j: (i, 0)),
                  pl.BlockSpec((k, tn), lambda i, j: (0, c0 + j))],
        out_specs=[out] * len(out_dtypes),
        out_shape=[jax.ShapeDtypeStruct((m, n), dt) for dt in out_dtypes],
        compiler_params=_cparams(("parallel", "arbitrary"), 52), name="matmul")(a, w)


def _mm_resid_kernel(a_ref, w_ref, x_ref, o_ref):
    o_ref[...] = x_ref[...] + _dot(a_ref[...], w_ref[...])


def matmul_resid(a, w, x, tm, tn):
    m, k = a.shape
    n = w.shape[1]
    return pl.pallas_call(
        _mm_resid_kernel, grid=(m // tm, n // tn),
        in_specs=[pl.BlockSpec((tm, k), lambda i, j: (i, 0)),
                  pl.BlockSpec((k, tn), lambda i, j: (0, j)),
                  pl.BlockSpec((tm, tn), lambda i, j: (i, j))],
        out_specs=pl.BlockSpec((tm, tn), lambda i, j: (i, j)),
        out_shape=jax.ShapeDtypeStruct((m, n), F32),
        compiler_params=_cparams(("parallel", "arbitrary"), 52), name="matmul_resid")(a, w, x)


def _mm_merge_kernel(ya_ref, yb_ref, wa_ref, wb_ref, ga_ref, gb_ref, o_ref):
    a = _dot(ya_ref[...], wa_ref[...])
    b = _dot(yb_ref[...], wb_ref[...])
    o_ref[...] = (jax.nn.sigmoid(ga_ref[...]) * a + jax.nn.sigmoid(gb_ref[...]) * b).astype(o_ref.dtype)


def merge_branches(ya, yb, wa, wb, proj, ga_col, gb_col, tm, tn):
    m, ka = ya.shape
    kb = yb.shape[1]
    n = wa.shape[1]
    ga0, gb0 = ga_col // tn, gb_col // tn
    return pl.pallas_call(
        _mm_merge_kernel, grid=(m // tm, n // tn),
        in_specs=[pl.BlockSpec((tm, ka), lambda i, j: (i, 0)),
                  pl.BlockSpec((tm, kb), lambda i, j: (i, 0)),
                  pl.BlockSpec((ka, tn), lambda i, j: (0, j)),
                  pl.BlockSpec((kb, tn), lambda i, j: (0, j)),
                  pl.BlockSpec((tm, tn), lambda i, j: (i, ga0 + j)),
                  pl.BlockSpec((tm, tn), lambda i, j: (i, gb0 + j))],
        out_specs=pl.BlockSpec((tm, tn), lambda i, j: (i, j)),
        out_shape=jax.ShapeDtypeStruct((m, n), BF16),
        compiler_params=_cparams(("parallel", "arbitrary"), 52), name="merge")(
            ya, yb, wa, wb, proj, proj)


def _conv_silu(ext, w_ref, b_ref, L):
    acc = b_ref[...] + w_ref[3:4, :] * ext[8:8 + L, :]
    acc = acc + w_ref[2:3, :] * ext[7:7 + L, :]
    acc = acc + w_ref[1:2, :] * ext[6:6 + L, :]
    acc = acc + w_ref[0:1, :] * ext[5:5 + L, :]
    return _silu(acc)


def _pair_terms(pr, x, dt, acs, acs_t, last_t, cb, ch, mask, hpn, lo_half):
    ra, rb = 2 * pr, 2 * pr + 1
    cola, colb = acs[:, ra:ra + 1], acs[:, rb:rb + 1]
    rowa, rowb = acs_t[ra:ra + 1, :], acs_t[rb:rb + 1, :]
    ma = cb * jnp.exp(jnp.where(mask, cola - rowa, NEG_BIG))
    mb = cb * jnp.exp(jnp.where(mask, colb - rowb, NEG_BIG))
    mcat = jnp.concatenate([ma.astype(BF16), mb.astype(BF16)], axis=1)
    xp = x[:, pr * 128:(pr + 1) * 128]
    xdt = xp * jnp.where(lo_half, dt[:, ra:ra + 1], dt[:, rb:rb + 1])
    rhs = jnp.concatenate([jnp.where(lo_half, xdt, 0.0), jnp.where(lo_half, 0.0, xdt)],
                          axis=0).astype(BF16)
    yd = _dot(mcat, rhs)
    eap = jnp.where(lo_half, jnp.exp(cola), jnp.exp(colb))
    dsk = jnp.where(lo_half[0:1, :], hpn[2:3, ra:ra + 1], hpn[2:3, rb:rb + 1])
    y = yd + ch[:, pr * 128:(pr + 1) * 128] * eap + xp * dsk
    return y, xdt, eap


def _ssd_prompt_kernel(xs_ref, b_ref, c_ref, z_ref, dtn_ref, dtt_ref,
                       cwx_ref, cwb_ref, cwc_ref, cbx_ref, cbb_ref, cbc_ref,
                       hpn_ref, hpt_ref, nw_ref, lin_ref, uin_ref,
                       y_ref, hout_ref, extx, extb, extc, hs, *, L):
    c = pl.program_id(1)

    @pl.when(c == 0)
    def _():
        extx[0:8, :] = jnp.zeros((8, GROUP_W), F32)
        extb[0:8, :] = jnp.zeros((8, D_STATE), F32)
        extc[0:8, :] = jnp.zeros((8, D_STATE), F32)
        hs[...] = jnp.zeros_like(hs)

    extx[8:8 + L, :] = xs_ref[...]
    extb[8:8 + L, :] = b_ref[...]
    extc[8:8 + L, :] = c_ref[...]
    x = _conv_silu(extx, cwx_ref, cbx_ref, L)
    bm = _conv_silu(extb, cwb_ref, cbb_ref, L).astype(BF16)
    cm = _conv_silu(extc, cwc_ref, cbc_ref, L).astype(BF16)
    extx[0:8, :] = extx[L:L + 8, :]
    extb[0:8, :] = extb[L:L + 8, :]
    extc[0:8, :] = extc[L:L + 8, :]

    hpn = hpn_ref[0]
    hpt = hpt_ref[0]
    dt = _softplus(dtn_ref[0] + hpn[0:1, :])
    d_a = dt * (-jnp.exp(hpn[1:2, :]))
    d_at = _softplus(dtt_ref[0] + hpt[:, 0:1]) * (-jnp.exp(hpt[:, 1:2]))
    lin = lin_ref[...]
    uin = uin_ref[...]
    acs = sum(_dot(lin, p) for p in _split3(d_a))
    acs_t = sum(_dot(p, uin) for p in _split3(d_at))

    rows = lax.broadcasted_iota(jnp.int32, (L, L), 0)
    cols = lax.broadcasted_iota(jnp.int32, (L, L), 1)
    causal = rows >= cols
    lo_half = lax.broadcasted_iota(jnp.int32, (L, 128), 1) < HEAD_P
    sub_lo = lax.broadcasted_iota(jnp.int32, (128, D_STATE), 0) < HEAD_P

    cb = _dot_nt(cm, bm)
    ch = _dot_nt(cm, hs[...].astype(BF16))

    ys = []
    for pr in range(GROUP_HEADS // 2):
        ra, rb = 2 * pr, 2 * pr + 1
        y, xdt, _ = _pair_terms(pr, x, dt, acs, acs_t, None, cb, ch, causal, hpn, lo_half)
        ys.append(y)
        lasta = acs_t[ra:ra + 1, L - 1:L]
        lastb = acs_t[rb:rb + 1, L - 1:L]
        dend = jnp.where(lo_half, jnp.exp(lasta - acs[:, ra:ra + 1]),
                         jnp.exp(lastb - acs[:, rb:rb + 1]))
        s_new = _dot_tn((xdt * dend).astype(BF16), bm)
        cd = jnp.where(sub_lo, jnp.exp(lasta), jnp.exp(lastb))
        sl = slice(pr * 128, (pr + 1) * 128)
        hs[sl, :] = hs[sl, :] * cd + s_new

    yg = jnp.concatenate(ys, axis=1)
    u = yg * _silu(z_ref[...])
    u = u * lax.rsqrt(jnp.mean(u * u, axis=-1, keepdims=True) + EPS)
    y_ref[...] = (u * nw_ref[...]).astype(y_ref.dtype)

    @pl.when(c == pl.num_programs(1) - 1)
    def _():
        hout_ref[...] = hs[...]


def _tri_incl(n):
    return np.tril(np.ones((n, n), np.float32))


def ssd_prompt(proj, t, dt_n, dt_t, conv_w, conv_b, hpn, hpt, norm_w, L):
    g = dt_n.shape[0]
    nc = t // L
    d_inner = g * GROUP_W
    xb = d_inner // GROUP_W
    bb = (2 * d_inner) // D_STATE
    cbk = bb + g
    lin = jnp.asarray(_tri_incl(L), BF16)
    uin = jnp.asarray(_tri_incl(L).T, BF16)
    const = lambda gi, ci: (0, 0)
    kern = functools.partial(_ssd_prompt_kernel, L=L)
    return pl.pallas_call(
        kern, grid=(g, nc),
        in_specs=[
            pl.BlockSpec((L, GROUP_W), lambda gi, ci: (ci, xb + gi)),
            pl.BlockSpec((L, D_STATE), lambda gi, ci: (ci, bb + gi)),
            pl.BlockSpec((L, D_STATE), lambda gi, ci: (ci, cbk + gi)),
            pl.BlockSpec((L, GROUP_W), lambda gi, ci: (ci, gi)),
            pl.BlockSpec((1, L, 128), lambda gi, ci: (gi, ci, 0)),
            pl.BlockSpec((1, 16, L), lambda gi, ci: (gi, 0, ci)),
            pl.BlockSpec((4, GROUP_W), lambda gi, ci: (0, gi)),
            pl.BlockSpec((4, D_STATE), lambda gi, ci: (0, bb - xb * 4 + gi)),
            pl.BlockSpec((4, D_STATE), lambda gi, ci: (0, bb - xb * 4 + g + gi)),
            pl.BlockSpec((1, GROUP_W), lambda gi, ci: (0, gi)),
            pl.BlockSpec((1, D_STATE), lambda gi, ci: (0, bb - xb * 4 + gi)),
            pl.BlockSpec((1, D_STATE), lambda gi, ci: (0, bb - xb * 4 + g + gi)),
            pl.BlockSpec((1, 8, 128), lambda gi, ci: (gi, 0, 0)),
            pl.BlockSpec((1, 16, 128), lambda gi, ci: (gi, 0, 0)),
            pl.BlockSpec((1, GROUP_W), lambda gi, ci: (0, gi)),
            pl.BlockSpec((L, L), const),
            pl.BlockSpec((L, L), const),
        ],
        out_specs=[pl.BlockSpec((L, GROUP_W), lambda gi, ci: (ci, gi)),
                   pl.BlockSpec((GROUP_W, D_STATE), lambda gi, ci: (gi, 0))],
        out_shape=[jax.ShapeDtypeStruct((t, d_inner), BF16),
                   jax.ShapeDtypeStruct((d_inner, D_STATE), F32)],
        scratch_shapes=[pltpu.VMEM((L + 8, GROUP_W), F32), pltpu.VMEM((L + 8, D_STATE), F32),
                        pltpu.VMEM((L + 8, D_STATE), F32), pltpu.VMEM((GROUP_W, D_STATE), F32)],
        compiler_params=_cparams(("arbitrary", "arbitrary"), 40), name="ssd_prompt")(
            proj, proj, proj, proj, dt_n, dt_t, conv_w, conv_w, conv_w, conv_b, conv_b, conv_b,
            hpn, hpt, norm_w, lin, uin)


def _ssd_sample_kernel(shx_ref, shb_ref, shc_ref, z_ref, dtn_ref, dtt_ref,
                       cwx_ref, cwb_ref, cwc_ref, cbx_ref, cbb_ref, cbc_ref,
                       hpn_ref, hpt_ref, nw_ref, lseg_ref, useg_ref, sseg_ref, st_ref,
                       y_ref, hout_ref, yacc, cm_s, bm_s, ea_s, xwt_s, cdl_s, *, n_tok, seq):
    b = pl.program_id(1)

    @pl.when(b == 0)
    def _():
        def conv(sh_ref, w_ref, b_ref):
            acc = b_ref[...] + w_ref[0:1, :] * sh_ref[0]
            for k in range(1, 4):
                acc = acc + w_ref[k:k + 1, :] * sh_ref[k]
            return _silu(acc)

        x = conv(shx_ref, cwx_ref, cbx_ref)
        bm = conv(shb_ref, cwb_ref, cbb_ref).astype(BF16)
        cm = conv(shc_ref, cwc_ref, cbc_ref).astype(BF16)
        hpn = hpn_ref[0]
        hpt = hpt_ref[0]
        dt = _softplus(dtn_ref[0] + hpn[0:1, :])
        d_a = dt * (-jnp.exp(hpn[1:2, :]))
        d_at = _softplus(dtt_ref[0] + hpt[:, 0:1]) * (-jnp.exp(hpt[:, 1:2]))
        lseg = lseg_ref[...]
        useg = useg_ref[...]
        sseg = sseg_ref[...]
        pa, pat = _split3(d_a), _split3(d_at)
        acs = sum(_dot(lseg, p) for p in pa)
        tot = sum(_dot(sseg, p) for p in pa)
        acs_t = sum(_dot(p, useg) for p in pat)
        tot_t = sum(_dot(p, sseg) for p in pat)
        mask = lseg > 0
        lo_half = lax.broadcasted_iota(jnp.int32, (n_tok, 128), 1) < HEAD_P
        cb = _dot_nt(cm, bm)
        zero_ch = jnp.zeros((n_tok, GROUP_W), F32)
        ys, xws, eas = [], [], []
        for pr in range(GROUP_HEADS // 2):
            ra, rb = 2 * pr, 2 * pr + 1
            y, xdt, eap = _pair_terms(pr, x, dt, acs, acs_t, None, cb, zero_ch, mask, hpn, lo_half)
            dend = jnp.where(lo_half, jnp.exp(tot[:, ra:ra + 1] - acs[:, ra:ra + 1]),
                             jnp.exp(tot[:, rb:rb + 1] - acs[:, rb:rb + 1]))
            ys.append(y)
            xws.append(xdt * dend)
            eas.append(eap)
        yacc[...] = jnp.concatenate(ys, axis=1)
        ea_s[...] = jnp.concatenate(eas, axis=1)
        xwt_s[...] = jnp.transpose(jnp.concatenate(xws, axis=1)).astype(BF16)
        cm_s[...] = cm
        bm_s[...] = bm
        cdl_s[...] = jnp.concatenate(
            [jnp.broadcast_to(tot_t[r:r + 1, :], (HEAD_P, n_tok)) for r in range(GROUP_HEADS)], axis=0)

    h0 = st_ref[0]
    lo = b * seq
    rid = lax.broadcasted_iota(jnp.int32, (n_tok, GROUP_W), 0)
    in_rows = jnp.abs(2 * (rid - lo) - (seq - 1)) < seq
    lid = lax.broadcasted_iota(jnp.int32, (GROUP_W, n_tok), 1)
    in_lanes = jnp.abs(2 * (lid - lo) - (seq - 1)) < seq
    ch = _dot_nt(cm_s[...], h0.astype(BF16))
    yacc[...] += jnp.where(in_rows, ch * ea_s[...], 0.0)
    s_new = _dot(jnp.where(in_lanes, xwt_s[...], jnp.zeros_like(xwt_s[...])), bm_s[...])
    cdcol = jnp.sum(jnp.where(lid == lo, cdl_s[...], 0.0), axis=-1, keepdims=True)
    hout_ref[0] = h0 * jnp.exp(cdcol) + s_new

    @pl.when(b == pl.num_programs(1) - 1)
    def _():
        u = yacc[...] * _silu(z_ref[...])
        u = u * lax.rsqrt(jnp.mean(u * u, axis=-1, keepdims=True) + EPS)
        y_ref[...] = (u * nw_ref[...]).astype(y_ref.dtype)


def ssd_sample(sh, proj, z_row_block, dt_n, dt_t, conv_w, conv_b, hpn, hpt, norm_w, state, seq):
    g = dt_n.shape[0]
    n_tok = sh.shape[1]
    n_seq = n_tok // seq
    d_inner = g * GROUP_W
    xb4 = d_inner // D_STATE
    sid = np.arange(n_tok) // seq
    same = (sid[:, None] == sid[None, :]).astype(np.float32)
    lseg = jnp.asarray(same * _tri_incl(n_tok), BF16)
    useg = jnp.asarray((same * _tri_incl(n_tok)).T, BF16)
    sseg = jnp.asarray(same, BF16)
    const = lambda gi, bi: (0, 0)
    kern = functools.partial(_ssd_sample_kernel, n_tok=n_tok, seq=seq)
    return pl.pallas_call(
        kern, grid=(g, n_seq),
        in_specs=[
            pl.BlockSpec((4, n_tok, GROUP_W), lambda gi, bi: (0, 0, gi)),
            pl.BlockSpec((4, n_tok, D_STATE), lambda gi, bi: (0, 0, xb4 + gi)),
            pl.BlockSpec((4, n_tok, D_STATE), lambda gi, bi: (0, 0, xb4 + g + gi)),
            pl.BlockSpec((n_tok, GROUP_W), lambda gi, bi: (z_row_block, gi)),
            pl.BlockSpec((1, n_tok, 128), lambda gi, bi: (gi, 0, 0)),
            pl.BlockSpec((1, 16, n_tok), lambda gi, bi: (gi, 0, 0)),
            pl.BlockSpec((4, GROUP_W), lambda gi, bi: (0, gi)),
            pl.BlockSpec((4, D_STATE), lambda gi, bi: (0, xb4 + gi)),
            pl.BlockSpec((4, D_STATE), lambda gi, bi: (0, xb4 + g + gi)),
            pl.BlockSpec((1, GROUP_W), lambda gi, bi: (0, gi)),
            pl.BlockSpec((1, D_STATE), lambda gi, bi: (0, xb4 + gi)),
            pl.BlockSpec((1, D_STATE), lambda gi, bi: (0, xb4 + g + gi)),
            pl.BlockSpec((1, 8, 128), lambda gi, bi: (gi, 0, 0)),
            pl.BlockSpec((1, 16, 128), lambda gi, bi: (gi, 0, 0)),
            pl.BlockSpec((1, GROUP_W), lambda gi, bi: (0, gi)),
            pl.BlockSpec((n_tok, n_tok), const),
            pl.BlockSpec((n_tok, n_tok), const),
            pl.BlockSpec((n_tok, n_tok), const),
            pl.BlockSpec((1, GROUP_W, D_STATE), lambda gi, bi: (bi, gi, 0)),
        ],
        out_specs=[pl.BlockSpec((n_tok, GROUP_W), lambda gi, bi: (0, gi)),
                   pl.BlockSpec((1, GROUP_W, D_STATE), lambda gi, bi: (bi, gi, 0))],
        out_shape=[jax.ShapeDtypeStruct((n_tok, d_inner), BF16),
                   jax.ShapeDtypeStruct(state.shape, F32)],
        scratch_shapes=[pltpu.VMEM((n_tok, GROUP_W), F32), pltpu.VMEM((n_tok, D_STATE), BF16),
                        pltpu.VMEM((n_tok, D_STATE), BF16), pltpu.VMEM((n_tok, GROUP_W), F32),
                        pltpu.VMEM((GROUP_W, n_tok), BF16), pltpu.VMEM((GROUP_W, n_tok), F32)],
        compiler_params=_cparams(("arbitrary", "arbitrary"), 40), name="ssd_sample")(
            sh, sh, sh, proj, dt_n, dt_t, conv_w, conv_w, conv_w, conv_b, conv_b, conv_b,
            hpn, hpt, norm_w, lseg, useg, sseg, state)


def _sb_prompt_kernel(bias_ref, q_ref, k_ref, v_ref, u_ref, o_ref, *, blk, hb):
    g = pl.program_id(0)
    i = pl.program_id(1)
    umat = u_ref[...]
    rows = lax.broadcasted_iota(jnp.int32, (blk, blk), 0)
    cols = lax.broadcasted_iota(jnp.int32, (blk, blk), 1)
    strict = cols < rows

    def block(j, carry, masked):
        start = pl.multiple_of(j * blk, blk)
        heads = range(hb)
        hsl = [slice(h * SB_DH, (h + 1) * SB_DH) for h in heads]
        zs = [_dot_nt(q_ref[:, hsl[h]], k_ref[pl.ds(start, blk), hsl[h]])
              + bias_ref[g * hb + h] * LOG2E for h in heads]
        sps = [_softplus2(z) for z in zs]
        if masked:
            sps = [jnp.where(strict, sp, 0.0) for sp in sps]
        sufs = [_dot(sp.astype(BF16), umat) for sp in sps]
        ws = [jnp.exp2(zs[h] - sps[h] - sufs[h] - carry[h][1]) for h in heads]
        if masked:
            ws = [jnp.where(strict, w, 0.0) for w in ws]
        accs = [carry[h][0] + _dot(ws[h].astype(BF16), v_ref[pl.ds(start, blk), hsl[h]])
                for h in heads]
        runs = [carry[h][1] + jnp.sum(sps[h], axis=-1, keepdims=True) for h in heads]
        return tuple(zip(accs, runs))

    carry = tuple((jnp.zeros((blk, SB_DH), F32), jnp.zeros((blk, 1), F32)) for _ in range(hb))
    carry = block(i, carry, True)
    carry = lax.fori_loop(0, i, lambda t, c: block(i - 1 - t, c, False), carry)
    for h in range(hb):
        o_ref[:, h * SB_DH:(h + 1) * SB_DH] = carry[h][0].astype(o_ref.dtype)


def _tri_later(n):
    return np.tril(np.ones((n, n), np.float32), -1)


def sb_prompt(q, k, v, t, bias, blk, hb):
    width = q.shape[1]
    gw = hb * SB_DH
    umat = jnp.asarray(_tri_later(blk), BF16)
    kern = functools.partial(_sb_prompt_kernel, blk=blk, hb=hb)
    return pl.pallas_call(
        kern, grid=(width // gw, t // blk),
        in_specs=[
            pl.BlockSpec(memory_space=pltpu.SMEM),
            pl.BlockSpec((blk, gw), lambda g, i: (i, g)),
            pl.BlockSpec((t, gw), lambda g, i: (0, g), pipeline_mode=pl.Buffered(1)),
            pl.BlockSpec((t, gw), lambda g, i: (0, g), pipeline_mode=pl.Buffered(1)),
            pl.BlockSpec((blk, blk), lambda g, i: (0, 0)),
        ],
        out_specs=pl.BlockSpec((blk, gw), lambda g, i: (i, g)),
        out_shape=jax.ShapeDtypeStruct((t, width), BF16),
        compiler_params=_cparams(("arbitrary", "arbitrary"), 48), name="sb_prompt")(
            bias, q, k, v, umat)


def _sb_sample_kernel(pt_ref, qbd_ref, brow_ref, kn_ref, vn_ref, tn_ref, tp_ref,
                      k0, k1, k2, k3, v0, v1, v2, v3,
                      o_ref, acc, run, kc, vc, *, n_heads, seq, page):
    s = pl.program_id(1)
    qbd = qbd_ref[0]
    brow = brow_ref[...]
    nq = n_heads * 8

    def process(kb, vb, tmat, mask):
        z = _dot(kb, qbd) + brow
        sp = _softplus2(z)
        if mask is not None:
            sp = jnp.where(mask, sp, 0.0)
        hi, lo = _split2(sp)
        suffix = _dot(tmat, hi) + _dot(tmat, lo)
        w = jnp.exp2(z - sp - suffix - run[...])
        if mask is not None:
            w = jnp.where(mask, w, 0.0)
        acc[...] += _dot_tn(w.astype(BF16), vb)
        run[...] += jnp.sum(sp, axis=0, keepdims=True)

    @pl.when(s == 0)
    def _():
        acc[...] = jnp.zeros_like(acc)
        run[...] = jnp.zeros_like(run)
        fill = jnp.zeros((page - 8, n_heads * SB_DH), F32)
        knew = jnp.concatenate([kn_ref[0], fill], axis=0).astype(BF16)
        vnew = jnp.concatenate([vn_ref[0], fill], axis=0).astype(BF16)
        key = lax.broadcasted_iota(jnp.int32, (page, nq), 0)
        qpos = lax.broadcasted_iota(jnp.int32, (page, nq), 1) % 8
        process(knew, vnew, tn_ref[...], key < qpos)

    for u, (kp, vp) in enumerate(((k0, v0), (k1, v1), (k2, v2), (k3, v3))):
        r0 = (3 - u) * page
        for h in range(n_heads):
            hs = slice(h * SB_DH, (h + 1) * SB_DH)
            kc[r0:r0 + page, hs] = kp[0, pl.ds(h, page, stride=n_heads), :].astype(BF16)
            vc[r0:r0 + page, hs] = vp[0, pl.ds(h, page, stride=n_heads), :].astype(BF16)
    process(kc[...], vc[...], tp_ref[...], None)

    @pl.when(s == pl.num_programs(1) - 1)
    def _():
        for h in range(n_heads):
            o_ref[0, :, h * SB_DH:(h + 1) * SB_DH] = (
                acc[h * 8:h * 8 + seq, h * SB_DH:(h + 1) * SB_DH].astype(o_ref.dtype))


def sb_sample(q_s, k_s, v_s, cache_k, cache_v, page_table, bias, n_heads, seq):
    n_seq, n_pages = page_table.shape
    page = cache_k.shape[1] // n_heads
    width = n_heads * SB_DH
    nq = n_heads * 8
    ppb = 4
    steps = n_pages // ppb
    q4 = jnp.pad(q_s.reshape(n_seq, seq, n_heads, SB_DH), ((0, 0), (0, 8 - seq), (0, 0), (0, 0)))
    eye = jnp.eye(n_heads, dtype=q_s.dtype)
    qbd = jnp.einsum("bthd,hg->bhdgt", q4, eye).reshape(n_seq, width, nq).astype(BF16)
    brow = jnp.repeat(bias * LOG2E, 8).reshape(1, nq)
    pad = lambda a: jnp.pad(a.reshape(n_seq, seq, width), ((0, 0), (0, 8 - seq), (0, 0)))
    tn = jnp.asarray(_tri_later(page).T, BF16)
    tp = jnp.asarray(_tri_later(ppb * page).T, BF16)

    def page_spec(u):
        return pl.BlockSpec((1, page * n_heads, SB_DH),
                            lambda b, s, pt: (pt[b, n_pages - 1 - (ppb * s + u)], 0, 0))

    kern = functools.partial(_sb_sample_kernel, n_heads=n_heads, seq=seq, page=page)
    grid_spec = pltpu.PrefetchScalarGridSpec(
        num_scalar_prefetch=1, grid=(n_seq, steps),
        in_specs=[
            pl.BlockSpec((1, width, nq), lambda b, s, pt: (b, 0, 0)),
            pl.BlockSpec((1, nq), lambda b, s, pt: (0, 0)),
            pl.BlockSpec((1, 8, width), lambda b, s, pt: (b, 0, 0)),
            pl.BlockSpec((1, 8, width), lambda b, s, pt: (b, 0, 0)),
            pl.BlockSpec((page, page), lambda b, s, pt: (0, 0)),
            pl.BlockSpec((ppb * page, ppb * page), lambda b, s, pt: (0, 0)),
        ] + [page_spec(u) for u in range(ppb)] + [page_spec(u) for u in range(ppb)],
        out_specs=pl.BlockSpec((1, seq, width), lambda b, s, pt: (b, 0, 0)),
        scratch_shapes=[pltpu.VMEM((nq, width), F32), pltpu.VMEM((1, nq), F32),
                        pltpu.VMEM((ppb * page, width), BF16), pltpu.VMEM((ppb * page, width), BF16)])
    return pl.pallas_call(
        kern, grid_spec=grid_spec,
        out_shape=jax.ShapeDtypeStruct((n_seq, seq, width), BF16),
        compiler_params=_cparams(("arbitrary", "arbitrary"), 48), name="sb_sample")(
            page_table, qbd, brow, pad(k_s), pad(v_s), tn, tp,
            *([cache_k] * ppb), *([cache_v] * ppb))


def _top16_rows(s):
    n = s.shape[0]
    rid = lax.broadcasted_iota(jnp.int32, s.shape, 0)
    out = []
    for _ in range(PEER_K):
        m = jnp.max(s, axis=0, keepdims=True)
        first = jnp.min(jnp.where(s == m, rid, n), axis=0, keepdims=True)
        out.append(m)
        s = jnp.where(rid == first, -jnp.inf, s)
    return out


def _peer_topk_kernel(q_ref, keys_ref, s_ref, st_ref, *, n_heads):
    tm = q_ref.shape[0]
    for h in range(n_heads):
        tops = []
        for c in range(2):
            qh = q_ref[:, (2 * h + c) * 128:(2 * h + c + 1) * 128]
            sc = _dot_nt(keys_ref[h, c].astype(BF16), qh)
            s_ref[c, h] = sc
            tops.append(_top16_rows(sc))
        sv2 = jnp.concatenate(tops[1], axis=0)
        cand = jnp.concatenate([tops[0][a] + sv2 for a in range(PEER_K)], axis=0)
        best = _top16_rows(cand)
        m = best[0]
        zsum = sum(jnp.exp(v - m) for v in best)
        max2 = tops[1][0]
        st_ref[h] = jnp.concatenate(
            [best[PEER_K - 1], m + jnp.log(zsum) - max2, max2, jnp.zeros((5, tm), F32)], axis=0)


def peer_topk(qp, sub_keys, tm):
    r = qp.shape[0]
    n_heads = sub_keys.shape[0]
    kern = functools.partial(_peer_topk_kernel, n_heads=n_heads)
    return pl.pallas_call(
        kern, grid=(r // tm,),
        in_specs=[pl.BlockSpec((tm, qp.shape[1]), lambda i: (i, 0)),
                  pl.BlockSpec(sub_keys.shape, lambda i: (0, 0, 0, 0))],
        out_specs=[pl.BlockSpec((2, n_heads, PEER_KEYS, tm), lambda i: (0, 0, 0, i)),
                   pl.BlockSpec((n_heads, 8, tm), lambda i: (0, 0, i))],
        out_shape=[jax.ShapeDtypeStruct((2, n_heads, PEER_KEYS, r), F32),
                   jax.ShapeDtypeStruct((n_heads, 8, r), F32)],
        compiler_params=_cparams(("parallel",), 40), name="peer_topk")(qp, sub_keys)


def _peer_dense_kernel(hn_ref, u_ref, vlo_ref, vhi_ref, s1_ref, s2_ref, st_ref, o_ref, p_prev, e2_s,
                       *, n_heads, half, n_trips):
    k = pl.program_id(1)

    @pl.when(k == 0)
    def _():
        o_ref[...] = jnp.zeros_like(o_ref)
        p_prev[...] = jnp.zeros_like(p_prev)
        for h in range(n_heads):
            e2_s[h] = jnp.exp(s2_ref[h] - st_ref[h, 2:3, :])

    live = (k < n_trips - 1).astype(F32)
    hn = hn_ref[...]

    hids = [_dot_nt(u_ref[part * half:(part + 1) * half, :], hn) for part in range(2)]
    first = _dot_tn(p_prev[...], vlo_ref[...])

    def gate_act(part):
        hid = hids[part]
        act = (0.5 * live) * hid * (1.0 + lax.erf(hid * (2.0 ** -0.5)))
        out = []
        for ii in range(half // PEER_KEYS):
            i_row = part * (half // PEER_KEYS) + ii
            wsum = None
            for h in range(n_heads):
                s1 = s1_ref[i_row, h:h + 1, :]
                sel = (s1 + s2_ref[h]) >= st_ref[h, 0:1, :]
                g = jnp.where(sel, jnp.exp(s1 - st_ref[h, 1:2, :]) * e2_s[h], 0.0)
                wsum = g if wsum is None else wsum + g
            out.append((wsum * act[ii * PEER_KEYS:(ii + 1) * PEER_KEYS, :]).astype(BF16))
        return jnp.concatenate(out, axis=0)

    pa = gate_act(0)
    second = _dot_tn(pa, vhi_ref[...])
    p_prev[...] = gate_act(1)
    o_ref[...] += first + second


def peer_dense(hn, eu, ev, s1, s2, st, tm, half):
    r, d = hn.shape
    n_exp = eu.shape[0]
    n_heads = s2.shape[0]
    n_pairs = n_exp // (2 * half)
    n_half = n_exp // half
    kern = functools.partial(_peer_dense_kernel, n_heads=n_heads, half=half, n_trips=n_pairs + 1)
    once = pl.Buffered(1)
    return pl.pallas_call(
        kern, grid=(r // tm, n_pairs + 1),
        in_specs=[pl.BlockSpec((tm, d), lambda i, k: (i, 0), pipeline_mode=once),
                  pl.BlockSpec((2 * half, d), lambda i, k: (jnp.minimum(k, n_pairs - 1), 0)),
                  pl.BlockSpec((half, d), lambda i, k: (jnp.maximum(2 * k - 1, 0), 0)),
                  pl.BlockSpec((half, d), lambda i, k: (jnp.minimum(2 * k, n_half - 1), 0)),
                  pl.BlockSpec((2 * half // PEER_KEYS, n_heads, tm),
                               lambda i, k: (jnp.minimum(k, n_pairs - 1), 0, i)),
                  pl.BlockSpec((n_heads, PEER_KEYS, tm), lambda i, k: (0, 0, i), pipeline_mode=once),
                  pl.BlockSpec((n_heads, 8, tm), lambda i, k: (0, 0, i), pipeline_mode=once)],
        out_specs=pl.BlockSpec((tm, d), lambda i, k: (i, 0), pipeline_mode=once),
        out_shape=jax.ShapeDtypeStruct((r, d), F32),
        scratch_shapes=[pltpu.VMEM((half, tm), BF16), pltpu.VMEM((n_heads, PEER_KEYS, tm), F32)],
        compiler_params=_cparams(("parallel", "arbitrary"), 56), name="peer_dense")(
            hn, eu, ev, ev, s1, s2, st)


def kernel(x_prompt, x_sample, cache_k, cache_v, state_ssm, state_conv, page_table, norm_mix, w_in,
           conv_w, conv_b, dt_bias, a_log, d_skip, ssm_norm_w, sb_bias, w_branch_a, w_branch_b,
           w_out, norm_ffn, w_query, sub_keys, expert_u, expert_v, norm_final):
    depth = w_in.shape[0]
    assert depth == 1 and x_prompt.shape[0] == 1
    _, t, d = x_prompt.shape
    n_seq, seq, _ = x_sample.shape
    n_tok = n_seq * seq
    r = t + n_tok
    n_ssm_heads = dt_bias.shape[1]
    d_inner = n_ssm_heads * HEAD_P
    g = n_ssm_heads // GROUP_HEADS
    conv_dim = conv_w.shape[2]
    n_sb = sb_bias.shape[1]
    sbw = n_sb * SB_DH
    assert t % 256 == 0 and n_tok == 128 and r % 640 == 0

    wl = w_in[0]
    dt0 = d_inner + conv_dim
    w_main = jnp.concatenate([wl[:, :dt0], wl[:, dt0 + n_ssm_heads:]], axis=1).astype(BF16)
    w_dt = jnp.pad(wl[:, dt0:dt0 + n_ssm_heads], ((0, 0), (0, 128 - n_ssm_heads))).astype(BF16)
    q_col = dt0
    k_col, v_col = q_col + sbw, q_col + 2 * sbw
    ga_col = q_col + 3 * sbw
    gb_col = ga_col + d

    x_all = jnp.concatenate([x_prompt[0], x_sample.reshape(n_tok, d)], axis=0)
    xn = rmsnorm(x_all, norm_mix[0], BF16)
    tm5 = r // 5
    (proj,) = matmul(xn, w_main, [F32], tm5, 512, col0=0, n=dt0)
    (q_bf,) = matmul(xn, w_main, [BF16], tm5, 512, col0=q_col, n=sbw, scale=SB_DH ** -0.5 * LOG2E)
    k_f, k_bf = matmul(xn, w_main, [F32, BF16], tm5, 512, col0=k_col, n=sbw)
    v_f, v_bf = matmul(xn, w_main, [F32, BF16], tm5, 512, col0=v_col, n=sbw)
    (gates,) = matmul(xn, w_main, [F32], tm5, 512, col0=ga_col, n=2 * d)
    dt_raw = matmul(xn, w_dt, [F32], tm5, 128)[0][:, :n_ssm_heads]

    def head_layouts(v):
        rows = v.shape[0]
        vg = v.reshape(rows, g, GROUP_HEADS).transpose(1, 0, 2)
        nat = jnp.pad(vg, ((0, 0), (0, 0), (0, 128 - GROUP_HEADS)))
        tr = jnp.pad(vg.transpose(0, 2, 1), ((0, 0), (0, 16 - GROUP_HEADS), (0, 0)))
        return nat, tr

    hp = jnp.stack([dt_bias[0], a_log[0], d_skip[0]], axis=0)
    hp_g = hp.reshape(3, g, GROUP_HEADS).transpose(1, 0, 2)
    hpn = jnp.pad(hp_g, ((0, 0), (0, 5), (0, 128 - GROUP_HEADS)))
    hpt = jnp.pad(hp_g.transpose(0, 2, 1), ((0, 0), (0, 16 - GROUP_HEADS), (0, 125)))
    nw = ssm_norm_w[0].reshape(1, d_inner)
    cw, cbias = conv_w[0], conv_b[0].reshape(1, conv_dim)

    dtn_p, dtt_p = head_layouts(dt_raw[:t])
    ya_p, ssm_p = ssd_prompt(proj, t, dtn_p, dtt_p, cw, cbias, hpn, hpt, nw, L=256)

    xbc_s = proj[t:, d_inner:d_inner + conv_dim].reshape(n_seq, seq, conv_dim)
    ext_s = jnp.concatenate([state_conv[0], xbc_s], axis=1)
    sh = jnp.stack([ext_s[:, k:k + seq].reshape(n_tok, conv_dim) for k in range(4)], axis=0)
    dtn_s, dtt_s = head_layouts(dt_raw[t:])
    ya_s, ssm_s = ssd_sample(sh, proj, t // n_tok, dtn_s, dtt_s, cw, cbias, hpn, hpt, nw,
                             state_ssm[0].reshape(n_seq, d_inner, D_STATE), seq)

    yb_p = sb_prompt(q_bf, k_bf, v_bf, t, sb_bias[0], blk=256, hb=8)
    k_s, v_s = k_f[t:], v_f[t:]
    n_phys, page = cache_k.shape[1], cache_k.shape[2]
    yb_s = sb_sample(q_bf[t:], k_s, v_s, cache_k.reshape(n_phys, page * n_sb, SB_DH),
                     cache_v.reshape(n_phys, page * n_sb, SB_DH), page_table, sb_bias[0], n_sb, seq)

    ya = jnp.concatenate([ya_p, ya_s], axis=0)
    yb = jnp.concatenate([yb_p, yb_s.reshape(n_tok, sbw)], axis=0)
    merged = merge_branches(ya, yb, w_branch_a[0].astype(BF16), w_branch_b[0].astype(BF16),
                            gates, 0, d, tm=640, tn=512)
    h1 = matmul_resid(merged, w_out[0].astype(BF16), x_all, tm=640, tn=512)

    hn = rmsnorm(h1, norm_ffn[0], BF16)
    (qp,) = matmul(hn, w_query[0].astype(BF16), [BF16], tm5, 512)
    scores, stats = peer_topk(qp, sub_keys[0], tm=128)
    s1 = scores[0].transpose(1, 0, 2)
    peer = peer_dense(hn, expert_u[0].astype(BF16), expert_v[0].astype(BF16), s1, scores[1], stats,
                      tm=640, half=256)
    y_all = rmsnorm(h1, norm_final, F32, add=peer)

    y_prompt = y_all[:t].reshape(1, t, d)
    y_sample = y_all[t:].reshape(n_seq, seq, d)
    k_prompt = k_f[:t].reshape(1, 1, t, n_sb, SB_DH)
    v_prompt = v_f[:t].reshape(1, 1, t, n_sb, SB_DH)
    ssm_prompt = ssm_p.reshape(1, 1, n_ssm_heads, HEAD_P, D_STATE)
    conv_prompt = proj[t - 3:t, d_inner:d_inner + conv_dim].reshape(1, 1, 3, conv_dim)
    k_sample = k_s.reshape(1, n_seq, seq, n_sb, SB_DH)
    v_sample = v_s.reshape(1, n_seq, seq, n_sb, SB_DH)
    ssm_sample = ssm_s.reshape(1, n_seq, n_ssm_heads, HEAD_P, D_STATE)
    conv_sample = ext_s[:, seq:].reshape(1, n_seq, 3, conv_dim)
    return (y_prompt, y_sample, k_prompt, v_prompt, ssm_prompt, conv_prompt,
            k_sample, v_sample, ssm_sample, conv_sample)
```

```python
import functools
import math

import numpy as np
import jax
import jax.numpy as jnp
from jax import lax
from jax.experimental import pallas as pl
from jax.experimental.pallas import tpu as pltpu

F32 = jnp.float32
BF16 = jnp.bfloat16
EPS = 1e-6
NEG_BIG = -1e30

HEAD_P = 64
GROUP_HEADS = 8
GROUP_W = HEAD_P * GROUP_HEADS
D_STATE = 128
SB_DH = 128
PEER_K = 16
PEER_KEYS = 128


def _cparams(sem, vmem_mb):
    return pltpu.CompilerParams(dimension_semantics=sem,
                                vmem_limit_bytes=vmem_mb * 1024 * 1024)


def _softplus(x):
    return jnp.maximum(x, 0.0) + jnp.log1p(jnp.exp(-jnp.abs(x)))


LOG2E = 1.4426950408889634


def _softplus2(x):
    return jnp.where(x > 64.0, x, jnp.log(1.0 + jnp.exp2(x)) * LOG2E)


def _silu(x):
    return x * jax.nn.sigmoid(x)


def _split2(x):
    hi = x.astype(BF16)
    lo = (x - hi.astype(F32)).astype(BF16)
    return hi, lo


def _split3(x):
    hi = x.astype(BF16)
    r = x - hi.astype(F32)
    mid = r.astype(BF16)
    lo = (r - mid.astype(F32)).astype(BF16)
    return hi, mid, lo


def _dot(a, b):
    return jnp.dot(a, b, preferred_element_type=F32)


def _dot_nt(a, b):
    return lax.dot_general(a, b, (((1,), (1,)), ((), ())), preferred_element_type=F32)


def _dot_tn(a, b):
    return lax.dot_general(a, b, (((0,), (0,)), ((), ())), preferred_element_type=F32)


def _rms_kernel(x_ref, w_ref, o_ref):
    x = x_ref[...]
    r = x * lax.rsqrt(jnp.mean(x * x, axis=-1, keepdims=True) + EPS)
    o_ref[...] = (r * w_ref[...]).astype(o_ref.dtype)


def rmsnorm(x, w, out_dtype, tm=128):
    m, d = x.shape
    row = pl.BlockSpec((tm, d), lambda i: (i, 0))
    return pl.pallas_call(
        _rms_kernel, grid=(m // tm,),
        in_specs=[row, pl.BlockSpec((1, d), lambda i: (0, 0))], out_specs=row,
        out_shape=jax.ShapeDtypeStruct((m, d), out_dtype),
        compiler_params=_cparams(("parallel",), 40), name="rmsnorm")(x, w.reshape(1, d))


def _rms_add_split_kernel(x_ref, y_ref, w_ref, head_ref, tail_ref, *, head_blocks):
    i = pl.program_id(0)
    x = x_ref[...] + y_ref[...]
    r = x * lax.rsqrt(jnp.mean(x * x, axis=-1, keepdims=True) + EPS)
    out = r * w_ref[...]

    @pl.when(i < head_blocks)
    def _():
        head_ref[...] = out

    @pl.when(i >= head_blocks)
    def _():
        tail_ref[...] = out


def rmsnorm_add_split(x, y, w, n_head, tm=128):
    m, d = x.shape
    hb = n_head // tm
    row = pl.BlockSpec((tm, d), lambda i: (i, 0))
    return pl.pallas_call(
        functools.partial(_rms_add_split_kernel, head_blocks=hb), grid=(m // tm,),
        in_specs=[row, row, pl.BlockSpec((1, d), lambda i: (0, 0))],
        out_specs=[pl.BlockSpec((tm, d), lambda i: (jnp.minimum(i, hb - 1), 0)),
                   pl.BlockSpec((tm, d), lambda i: (jnp.maximum(i - hb, 0), 0))],
        out_shape=[jax.ShapeDtypeStruct((n_head, d), F32),
                   jax.ShapeDtypeStruct((m - n_head, d), F32)],
        compiler_params=_cparams(("arbitrary",), 40), name="rmsnorm_out")(x, y, w.reshape(1, d))


def _mm_kernel(a_ref, w_ref, *o_refs, scale):
    acc = _dot(a_ref[...], w_ref[...].astype(BF16))
    if scale is not None:
        acc = acc * scale
    for o_ref in o_refs:
        o_ref[...] = acc.astype(o_ref.dtype)


def matmul(a, w, out_dtypes, tm, tn, col0=0, n=None, scale=None, row0=0, m=None):
    k = a.shape[1]
    m = a.shape[0] if m is None else m
    n = w.shape[1] if n is None else n
    c0, r0 = col0 // tn, row0 // tm
    out = pl.BlockSpec((tm, tn), lambda i, j: (i, j))
    return pl.pallas_call(
        functools.partial(_mm_kernel, scale=scale), grid=(m // tm, n // tn),
        in_specs=[pl.BlockSpec((tm, k), lambda i, j: (r0 + i, 0)),
                  pl.BlockSpec((k, tn), lambda i, j: (0, c0 + j))],
        out_specs=[out] * len(out_dtypes),
        out_shape=[jax.ShapeDtypeStruct((m, n), dt) for dt in out_dtypes],
        compiler_params=_cparams(("parallel", "arbitrary"), 52), name="matmul")(a, w)


def _mm_resid_kernel(a_ref, w_ref, x_ref, o_ref):
    o_ref[...] = x_ref[...] + _dot(a_ref[...], w_ref[...])


def matmul_resid(a, w, x, tm, tn):
    m, k = a.shape
    n = w.shape[1]
    return pl.pallas_call(
        _mm_resid_kernel, grid=(m // tm, n // tn),
        in_specs=[pl.BlockSpec((tm, k), lambda i, j: (i, 0)),
                  pl.BlockSpec((k, tn), lambda i, j: (0, j)),
                  pl.BlockSpec((tm, tn), lambda i, j: (i, j))],
        out_specs=pl.BlockSpec((tm, tn), lambda i, j: (i, j)),
        out_shape=jax.ShapeDtypeStruct((m, n), F32),
        compiler_params=_cparams(("parallel", "arbitrary"), 52), name="matmul_resid")(a, w, x)


def _mm_merge_kernel(ya_ref, yb_ref, wa_ref, wb_ref, ga_ref, gb_ref, o_ref):
    a = _dot(ya_ref[...], wa_ref[...])
    b = _dot(yb_ref[...], wb_ref[...])
    o_ref[...] = (jax.nn.sigmoid(ga_ref[...]) * a + jax.nn.sigmoid(gb_ref[...]) * b).astype(o_ref.dtype)


def merge_branches(ya, yb, wa, wb, proj, ga_col, gb_col, tm, tn):
    m, ka = ya.shape
    kb = yb.shape[1]
    n = wa.shape[1]
    ga0, gb0 = ga_col // tn, gb_col // tn
    return pl.pallas_call(
        _mm_merge_kernel, grid=(m // tm, n // tn),
        in_specs=[pl.BlockSpec((tm, ka), lambda i, j: (i, 0)),
                  pl.BlockSpec((tm, kb), lambda i, j: (i, 0)),
                  pl.BlockSpec((ka, tn), lambda i, j: (0, j)),
                  pl.BlockSpec((kb, tn), lambda i, j: (0, j)),
                  pl.BlockSpec((tm, tn), lambda i, j: (i, ga0 + j)),
                  pl.BlockSpec((tm, tn), lambda i, j: (i, gb0 + j))],
        out_specs=pl.BlockSpec((tm, tn), lambda i, j: (i, j)),
        out_shape=jax.ShapeDtypeStruct((m, n), BF16),
        compiler_params=_cparams(("parallel", "arbitrary"), 52), name="merge")(
            ya, yb, wa, wb, proj, proj)


def _conv_silu(ext, w_ref, b_ref, L):
    acc = b_ref[...] + w_ref[3:4, :] * ext[8:8 + L, :]
    acc = acc + w_ref[2:3, :] * ext[7:7 + L, :]
    acc = acc + w_ref[1:2, :] * ext[6:6 + L, :]
    acc = acc + w_ref[0:1, :] * ext[5:5 + L, :]
    return _silu(acc)


def _pair_terms(pr, x, dt, acs, acs_t, last_t, cb, ch, mask, hpn, lo_half):
    ra, rb = 2 * pr, 2 * pr + 1
    cola, colb = acs[:, ra:ra + 1], acs[:, rb:rb + 1]
    rowa, rowb = acs_t[ra:ra + 1, :], acs_t[rb:rb + 1, :]
    ma = cb * jnp.exp(jnp.where(mask, cola - rowa, NEG_BIG))
    mb = cb * jnp.exp(jnp.where(mask, colb - rowb, NEG_BIG))
    mcat = jnp.concatenate([ma.astype(BF16), mb.astype(BF16)], axis=1)
    xp = x[:, pr * 128:(pr + 1) * 128]
    xdt = xp * jnp.where(lo_half, dt[:, ra:ra + 1], dt[:, rb:rb + 1])
    rhs = jnp.concatenate([jnp.where(lo_half, xdt, 0.0), jnp.where(lo_half, 0.0, xdt)],
                          axis=0).astype(BF16)
    yd = _dot(mcat, rhs)
    eap = jnp.where(lo_half, jnp.exp(cola), jnp.exp(colb))
    dsk = jnp.where(lo_half[0:1, :], hpn[2:3, ra:ra + 1], hpn[2:3, rb:rb + 1])
    y = yd + ch[:, pr * 128:(pr + 1) * 128] * eap + xp * dsk
    return y, xdt, eap


def _ssd_prompt_kernel(xs_ref, b_ref, c_ref, z_ref, dtn_ref, dtt_ref,
                       cwx_ref, cwb_ref, cwc_ref, cbx_ref, cbb_ref, cbc_ref,
                       hpn_ref, hpt_ref, nw_ref, lin_ref, uin_ref,
                       y_ref, hout_ref, extx, extb, extc, hs, *, L):
    c = pl.program_id(1)

    @pl.when(c == 0)
    def _():
        extx[0:8, :] = jnp.zeros((8, GROUP_W), F32)
        extb[0:8, :] = jnp.zeros((8, D_STATE), F32)
        extc[0:8, :] = jnp.zeros((8, D_STATE), F32)
        hs[...] = jnp.zeros_like(hs)

    extx[8:8 + L, :] = xs_ref[...]
    extb[8:8 + L, :] = b_ref[...]
    extc[8:8 + L, :] = c_ref[...]
    x = _conv_silu(extx, cwx_ref, cbx_ref, L)
    bm = _conv_silu(extb, cwb_ref, cbb_ref, L).astype(BF16)
    cm = _conv_silu(extc, cwc_ref, cbc_ref, L).astype(BF16)
    extx[0:8, :] = extx[L:L + 8, :]
    extb[0:8, :] = extb[L:L + 8, :]
    extc[0:8, :] = extc[L:L + 8, :]

    hpn = hpn_ref[0]
    hpt = hpt_ref[0]
    dt = _softplus(dtn_ref[0] + hpn[0:1, :])
    d_a = dt * (-jnp.exp(hpn[1:2, :]))
    d_at = _softplus(dtt_ref[0] + hpt[:, 0:1]) * (-jnp.exp(hpt[:, 1:2]))
    lin = lin_ref[...]
    uin = uin_ref[...]
    acs = sum(_dot(lin, p) for p in _split3(d_a))
    acs_t = sum(_dot(p, uin) for p in _split3(d_at))

    rows = lax.broadcasted_iota(jnp.int32, (L, L), 0)
    cols = lax.broadcasted_iota(jnp.int32, (L, L), 1)
    causal = rows >= cols
    lo_half = lax.broadcasted_iota(jnp.int32, (L, 128), 1) < HEAD_P
    sub_lo = lax.broadcasted_iota(jnp.int32, (128, D_STATE), 0) < HEAD_P

    cb = _dot_nt(cm, bm)
    ch = _dot_nt(cm, hs[...].astype(BF16))

    ys = []
    for pr in range(GROUP_HEADS // 2):
        ra, rb = 2 * pr, 2 * pr + 1
        y, xdt, _ = _pair_terms(pr, x, dt, acs, acs_t, None, cb, ch, causal, hpn, lo_half)
        ys.append(y)
        lasta = acs_t[ra:ra + 1, L - 1:L]
        lastb = acs_t[rb:rb + 1, L - 1:L]
        dend = jnp.where(lo_half, jnp.exp(lasta - acs[:, ra:ra + 1]),
                         jnp.exp(lastb - acs[:, rb:rb + 1]))
        s_new = _dot_tn((xdt * dend).astype(BF16), bm)
        cd = jnp.where(sub_lo, jnp.exp(lasta), jnp.exp(lastb))
        sl = slice(pr * 128, (pr + 1) * 128)
        hs[sl, :] = hs[sl, :] * cd + s_new

    yg = jnp.concatenate(ys, axis=1)
    u = yg * _silu(z_ref[...])
    u = u * lax.rsqrt(jnp.mean(u * u, axis=-1, keepdims=True) + EPS)
    y_ref[...] = (u * nw_ref[...]).astype(y_ref.dtype)

    @pl.when(c == pl.num_programs(1) - 1)
    def _():
        hout_ref[...] = hs[...]


def _tri_incl(n):
    return np.tril(np.ones((n, n), np.float32))


def ssd_prompt(proj, t, dt_n, dt_t, conv_w, conv_b, hpn, hpt, norm_w, L):
    g = dt_n.shape[0]
    nc = t // L
    d_inner = g * GROUP_W
    xb = d_inner // GROUP_W
    bb = (2 * d_inner) // D_STATE
    cbk = bb + g
    lin = jnp.asarray(_tri_incl(L), BF16)
    uin = jnp.asarray(_tri_incl(L).T, BF16)
    const = lambda gi, ci: (0, 0)
    kern = functools.partial(_ssd_prompt_kernel, L=L)
    return pl.pallas_call(
        kern, grid=(g, nc),
        in_specs=[
            pl.BlockSpec((L, GROUP_W), lambda gi, ci: (ci, xb + gi)),
            pl.BlockSpec((L, D_STATE), lambda gi, ci: (ci, bb + gi)),
            pl.BlockSpec((L, D_STATE), lambda gi, ci: (ci, cbk + gi)),
            pl.BlockSpec((L, GROUP_W), lambda gi, ci: (ci, gi)),
            pl.BlockSpec((1, L, 128), lambda gi, ci: (gi, ci, 0)),
            pl.BlockSpec((1, 16, L), lambda gi, ci: (gi, 0, ci)),
            pl.BlockSpec((4, GROUP_W), lambda gi, ci: (0, gi)),
            pl.BlockSpec((4, D_STATE), lambda gi, ci: (0, bb - xb * 4 + gi)),
            pl.BlockSpec((4, D_STATE), lambda gi, ci: (0, bb - xb * 4 + g + gi)),
            pl.BlockSpec((1, GROUP_W), lambda gi, ci: (0, gi)),
            pl.BlockSpec((1, D_STATE), lambda gi, ci: (0, bb - xb * 4 + gi)),
            pl.BlockSpec((1, D_STATE), lambda gi, ci: (0, bb - xb * 4 + g + gi)),
            pl.BlockSpec((1, 8, 128), lambda gi, ci: (gi, 0, 0)),
            pl.BlockSpec((1, 16, 128), lambda gi, ci: (gi, 0, 0)),
            pl.BlockSpec((1, GROUP_W), lambda gi, ci: (0, gi)),
            pl.BlockSpec((L, L), const),
            pl.BlockSpec((L, L), const),
        ],
        out_specs=[pl.BlockSpec((L, GROUP_W), lambda gi, ci: (ci, gi)),
                   pl.BlockSpec((GROUP_W, D_STATE), lambda gi, ci: (gi, 0))],
        out_shape=[jax.ShapeDtypeStruct((proj.shape[0], d_inner), BF16),
                   jax.ShapeDtypeStruct((d_inner, D_STATE), F32)],
        scratch_shapes=[pltpu.VMEM((L + 8, GROUP_W), F32), pltpu.VMEM((L + 8, D_STATE), F32),
                        pltpu.VMEM((L + 8, D_STATE), F32), pltpu.VMEM((GROUP_W, D_STATE), F32)],
        compiler_params=_cparams(("arbitrary", "arbitrary"), 40), name="ssd_prompt")(
            proj, proj, proj, proj, dt_n, dt_t, conv_w, conv_w, conv_w, conv_b, conv_b, conv_b,
            hpn, hpt, norm_w, lin, uin)


def _ssd_sample_kernel(shx_ref, shb_ref, shc_ref, z_ref, dtn_ref, dtt_ref,
                       cwx_ref, cwb_ref, cwc_ref, cbx_ref, cbb_ref, cbc_ref,
                       hpn_ref, hpt_ref, nw_ref, lseg_ref, useg_ref, sseg_ref, st_ref, y_all_ref,
                       y_ref, hout_ref, yacc, cm_s, bm_s, ea_s, xwt_s, cdl_s, *, n_tok, seq):
    b = pl.program_id(1)

    @pl.when(b == 0)
    def _():
        def conv(sh_ref, w_ref, b_ref):
            acc = b_ref[...] + w_ref[0:1, :] * sh_ref[0]
            for k in range(1, 4):
                acc = acc + w_ref[k:k + 1, :] * sh_ref[k]
            return _silu(acc)

        x = conv(shx_ref, cwx_ref, cbx_ref)
        bm = conv(shb_ref, cwb_ref, cbb_ref).astype(BF16)
        cm = conv(shc_ref, cwc_ref, cbc_ref).astype(BF16)
        hpn = hpn_ref[0]
        hpt = hpt_ref[0]
        dt = _softplus(dtn_ref[0] + hpn[0:1, :])
        d_a = dt * (-jnp.exp(hpn[1:2, :]))
        d_at = _softplus(dtt_ref[0] + hpt[:, 0:1]) * (-jnp.exp(hpt[:, 1:2]))
        lseg = lseg_ref[...]
        useg = useg_ref[...]
        sseg = sseg_ref[...]
        pa, pat = _split3(d_a), _split3(d_at)
        acs = sum(_dot(lseg, p) for p in pa)
        tot = sum(_dot(sseg, p) for p in pa)
        acs_t = sum(_dot(p, useg) for p in pat)
        tot_t = sum(_dot(p, sseg) for p in pat)
        mask = lseg > 0
        lo_half = lax.broadcasted_iota(jnp.int32, (n_tok, 128), 1) < HEAD_P
        cb = _dot_nt(cm, bm)
        zero_ch = jnp.zeros((n_tok, GROUP_W), F32)
        ys, xws, eas = [], [], []
        for pr in range(GROUP_HEADS // 2):
            ra, rb = 2 * pr, 2 * pr + 1
            y, xdt, eap = _pair_terms(pr, x, dt, acs, acs_t, None, cb, zero_ch, mask, hpn, lo_half)
            dend = jnp.where(lo_half, jnp.exp(tot[:, ra:ra + 1] - acs[:, ra:ra + 1]),
                             jnp.exp(tot[:, rb:rb + 1] - acs[:, rb:rb + 1]))
            ys.append(y)
            xws.append(xdt * dend)
            eas.append(eap)
        yacc[...] = jnp.concatenate(ys, axis=1)
        ea_s[...] = jnp.concatenate(eas, axis=1)
        xwt_s[...] = jnp.transpose(jnp.concatenate(xws, axis=1)).astype(BF16)
        cm_s[...] = cm
        bm_s[...] = bm
        cdl_s[...] = jnp.concatenate(
            [jnp.broadcast_to(tot_t[r:r + 1, :], (HEAD_P, n_tok)) for r in range(GROUP_HEADS)], axis=0)

    h0 = st_ref[0]
    lo = b * seq
    rid = lax.broadcasted_iota(jnp.int32, (n_tok, GROUP_W), 0)
    in_rows = jnp.abs(2 * (rid - lo) - (seq - 1)) < seq
    lid = lax.broadcasted_iota(jnp.int32, (GROUP_W, n_tok), 1)
    in_lanes = jnp.abs(2 * (lid - lo) - (seq - 1)) < seq
    ch = _dot_nt(cm_s[...], h0.astype(BF16))
    yacc[...] += jnp.where(in_rows, ch * ea_s[...], 0.0)
    s_new = _dot(jnp.where(in_lanes, xwt_s[...], jnp.zeros_like(xwt_s[...])), bm_s[...])
    cdcol = jnp.sum(jnp.where(lid == lo, cdl_s[...], 0.0), axis=-1, keepdims=True)
    hout_ref[0] = h0 * jnp.exp(cdcol) + s_new

    @pl.when(b == pl.num_programs(1) - 1)
    def _():
        u = yacc[...] * _silu(z_ref[...])
        u = u * lax.rsqrt(jnp.mean(u * u, axis=-1, keepdims=True) + EPS)
        y_ref[...] = (u * nw_ref[...]).astype(y_ref.dtype)


def ssd_sample(sh, proj, z_row_block, dt_n, dt_t, conv_w, conv_b, hpn, hpt, norm_w, state, seq,
               y_all):
    g = dt_n.shape[0]
    n_tok = sh.shape[1]
    n_seq = n_tok // seq
    d_inner = g * GROUP_W
    xb4 = d_inner // D_STATE
    sid = np.arange(n_tok) // seq
    same = (sid[:, None] == sid[None, :]).astype(np.float32)
    lseg = jnp.asarray(same * _tri_incl(n_tok), BF16)
    useg = jnp.asarray((same * _tri_incl(n_tok)).T, BF16)
    sseg = jnp.asarray(same, BF16)
    const = lambda gi, bi: (0, 0)
    kern = functools.partial(_ssd_sample_kernel, n_tok=n_tok, seq=seq)
    return pl.pallas_call(
        kern, grid=(g, n_seq),
        in_specs=[
            pl.BlockSpec((4, n_tok, GROUP_W), lambda gi, bi: (0, 0, gi)),
            pl.BlockSpec((4, n_tok, D_STATE), lambda gi, bi: (0, 0, xb4 + gi)),
            pl.BlockSpec((4, n_tok, D_STATE), lambda gi, bi: (0, 0, xb4 + g + gi)),
            pl.BlockSpec((n_tok, GROUP_W), lambda gi, bi: (z_row_block, gi)),
            pl.BlockSpec((1, n_tok, 128), lambda gi, bi: (gi, 0, 0)),
            pl.BlockSpec((1, 16, n_tok), lambda gi, bi: (gi, 0, 0)),
            pl.BlockSpec((4, GROUP_W), lambda gi, bi: (0, gi)),
            pl.BlockSpec((4, D_STATE), lambda gi, bi: (0, xb4 + gi)),
            pl.BlockSpec((4, D_STATE), lambda gi, bi: (0, xb4 + g + gi)),
            pl.BlockSpec((1, GROUP_W), lambda gi, bi: (0, gi)),
            pl.BlockSpec((1, D_STATE), lambda gi, bi: (0, xb4 + gi)),
            pl.BlockSpec((1, D_STATE), lambda gi, bi: (0, xb4 + g + gi)),
            pl.BlockSpec((1, 8, 128), lambda gi, bi: (gi, 0, 0)),
            pl.BlockSpec((1, 16, 128), lambda gi, bi: (gi, 0, 0)),
            pl.BlockSpec((1, GROUP_W), lambda gi, bi: (0, gi)),
            pl.BlockSpec((n_tok, n_tok), const),
            pl.BlockSpec((n_tok, n_tok), const),
            pl.BlockSpec((n_tok, n_tok), const),
            pl.BlockSpec((1, GROUP_W, D_STATE), lambda gi, bi: (bi, gi, 0)),
            pl.BlockSpec(memory_space=pl.ANY),
        ],
        out_specs=[pl.BlockSpec((n_tok, GROUP_W), lambda gi, bi: (z_row_block, gi)),
                   pl.BlockSpec((1, GROUP_W, D_STATE), lambda gi, bi: (bi, gi, 0))],
        out_shape=[jax.ShapeDtypeStruct(y_all.shape, BF16),
                   jax.ShapeDtypeStruct(state.shape, F32)],
        input_output_aliases={19: 0},
        scratch_shapes=[pltpu.VMEM((n_tok, GROUP_W), F32), pltpu.VMEM((n_tok, D_STATE), BF16),
                        pltpu.VMEM((n_tok, D_STATE), BF16), pltpu.VMEM((n_tok, GROUP_W), F32),
                        pltpu.VMEM((GROUP_W, n_tok), BF16), pltpu.VMEM((GROUP_W, n_tok), F32)],
        compiler_params=_cparams(("arbitrary", "arbitrary"), 40), name="ssd_sample")(
            sh, sh, sh, proj, dt_n, dt_t, conv_w, conv_w, conv_w, conv_b, conv_b, conv_b,
            hpn, hpt, norm_w, lseg, useg, sseg, state, y_all)


def _sb_prompt_kernel(bias_ref, q_ref, k_ref, v_ref, u_ref, o_ref, *, blk, hb):
    g = pl.program_id(0)
    i = pl.program_id(1)
    umat = u_ref[...]
    rows = lax.broadcasted_iota(jnp.int32, (blk, blk), 0)
    cols = lax.broadcasted_iota(jnp.int32, (blk, blk), 1)
    strict = cols < rows

    def block(j, carry, masked):
        start = pl.multiple_of(j * blk, blk)
        heads = range(hb)
        hsl = [slice(h * SB_DH, (h + 1) * SB_DH) for h in heads]
        zs = [_dot_nt(q_ref[:, hsl[h]], k_ref[pl.ds(start, blk), hsl[h]])
              + bias_ref[g * hb + h] * LOG2E for h in heads]
        sps = [_softplus2(z) for z in zs]
        if masked:
            sps = [jnp.where(strict, sp, 0.0) for sp in sps]
        sufs = [_dot(sp.astype(BF16), umat) for sp in sps]
        ws = [jnp.exp2(zs[h] - sps[h] - sufs[h] - carry[h][1]) for h in heads]
        if masked:
            ws = [jnp.where(strict, w, 0.0) for w in ws]
        accs = [carry[h][0] + _dot(ws[h].astype(BF16), v_ref[pl.ds(start, blk), hsl[h]])
                for h in heads]
        runs = [carry[h][1] + jnp.sum(sps[h], axis=-1, keepdims=True) for h in heads]
        return tuple(zip(accs, runs))

    carry = tuple((jnp.zeros((blk, SB_DH), F32), jnp.zeros((blk, 1), F32)) for _ in range(hb))
    carry = block(i, carry, True)
    carry = lax.fori_loop(0, i, lambda t, c: block(i - 1 - t, c, False), carry)
    for h in range(hb):
        o_ref[:, h * SB_DH:(h + 1) * SB_DH] = carry[h][0].astype(o_ref.dtype)


def _tri_later(n):
    return np.tril(np.ones((n, n), np.float32), -1)


def sb_prompt(q, k, v, t, bias, blk, hb):
    width = q.shape[1]
    gw = hb * SB_DH
    umat = jnp.asarray(_tri_later(blk), BF16)
    kern = functools.partial(_sb_prompt_kernel, blk=blk, hb=hb)
    return pl.pallas_call(
        kern, grid=(width // gw, t // blk),
        in_specs=[
            pl.BlockSpec(memory_space=pltpu.SMEM),
            pl.BlockSpec((blk, gw), lambda g, i: (i, g)),
            pl.BlockSpec((t, gw), lambda g, i: (0, g), pipeline_mode=pl.Buffered(1)),
            pl.BlockSpec((t, gw), lambda g, i: (0, g), pipeline_mode=pl.Buffered(1)),
            pl.BlockSpec((blk, blk), lambda g, i: (0, 0)),
        ],
        out_specs=pl.BlockSpec((blk, gw), lambda g, i: (i, g)),
        out_shape=jax.ShapeDtypeStruct((t, width), BF16),
        compiler_params=_cparams(("arbitrary", "arbitrary"), 48), name="sb_prompt")(
            bias, q, k, v, umat)


def _sb_sample_kernel(pt_ref, qbd_ref, brow_ref, kn_ref, vn_ref, tn_ref, tp_ref, *rest,
                      n_heads, seq, page, ppb, grp):
    k_pages, v_pages = rest[:ppb], rest[ppb:2 * ppb]
    o_ref, acc, run = rest[2 * ppb:2 * ppb + 3]
    bufs = rest[2 * ppb + 3:]
    s = pl.program_id(1)
    qbd = qbd_ref[0]
    brow = brow_ref[...]
    nq = n_heads * 8

    def process(kb, vb, tmat, mask):
        z = _dot(kb, qbd) + brow
        sp = _softplus2(z)
        if mask is not None:
            sp = jnp.where(mask, sp, 0.0)
        hi, lo = _split2(sp)
        suffix = _dot(tmat, hi) + _dot(tmat, lo)
        w = jnp.exp2(z - sp - suffix - run[...])
        if mask is not None:
            w = jnp.where(mask, w, 0.0)
        acc[...] += _dot_tn(w.astype(BF16), vb)
        run[...] += jnp.sum(sp, axis=0, keepdims=True)

    @pl.when(s == 0)
    def _():
        acc[...] = jnp.zeros_like(acc)
        run[...] = jnp.zeros_like(run)
        fill = jnp.zeros((page - 8, n_heads * SB_DH), F32)
        knew = jnp.concatenate([kn_ref[0], fill], axis=0).astype(BF16)
        vnew = jnp.concatenate([vn_ref[0], fill], axis=0).astype(BF16)
        key = lax.broadcasted_iota(jnp.int32, (page, nq), 0)
        qpos = lax.broadcasted_iota(jnp.int32, (page, nq), 1) % 8
        process(knew, vnew, tn_ref[...], key < qpos)

    for gi in range(ppb // grp):
        kc, vc = bufs[2 * gi], bufs[2 * gi + 1]
        for u in range(grp):
            kp, vp = k_pages[gi * grp + u], v_pages[gi * grp + u]
            r0 = (grp - 1 - u) * page
            for h in range(n_heads):
                hs = slice(h * SB_DH, (h + 1) * SB_DH)
                kc[r0:r0 + page, hs] = kp[0, pl.ds(h, page, stride=n_heads), :].astype(BF16)
                vc[r0:r0 + page, hs] = vp[0, pl.ds(h, page, stride=n_heads), :].astype(BF16)
        process(kc[...], vc[...], tp_ref[...], None)

    @pl.when(s == pl.num_programs(1) - 1)
    def _():
        for h in range(n_heads):
            o_ref[0, :, h * SB_DH:(h + 1) * SB_DH] = (
                acc[h * 8:h * 8 + seq, h * SB_DH:(h + 1) * SB_DH].astype(o_ref.dtype))


def sb_sample(q_s, k_s, v_s, cache_k, cache_v, page_table, bias, n_heads, seq):
    n_seq, n_pages = page_table.shape
    page = cache_k.shape[1] // n_heads
    width = n_heads * SB_DH
    nq = n_heads * 8
    grp = 4
    ppb = 8 if n_pages % 8 == 0 else 4
    steps = n_pages // ppb
    q4 = jnp.pad(q_s.reshape(n_seq, seq, n_heads, SB_DH), ((0, 0), (0, 8 - seq), (0, 0), (0, 0)))
    eye = jnp.eye(n_heads, dtype=q_s.dtype)
    qbd = jnp.einsum("bthd,hg->bhdgt", q4, eye).reshape(n_seq, width, nq).astype(BF16)
    brow = jnp.repeat(bias * LOG2E, 8).reshape(1, nq)
    pad = lambda a: jnp.pad(a.reshape(n_seq, seq, width), ((0, 0), (0, 8 - seq), (0, 0)))
    tn = jnp.asarray(_tri_later(page).T, BF16)
    tp = jnp.asarray(_tri_later(grp * page).T, BF16)

    def page_spec(u):
        return pl.BlockSpec((1, page * n_heads, SB_DH),
                            lambda b, s, pt: (pt[b, n_pages - 1 - (ppb * s + u)], 0, 0))

    kern = functools.partial(_sb_sample_kernel, n_heads=n_heads, seq=seq, page=page, ppb=ppb,
                             grp=grp)
    grid_spec = pltpu.PrefetchScalarGridSpec(
        num_scalar_prefetch=1, grid=(n_seq, steps),
        in_specs=[
            pl.BlockSpec((1, width, nq), lambda b, s, pt: (b, 0, 0)),
            pl.BlockSpec((1, nq), lambda b, s, pt: (0, 0)),
            pl.BlockSpec((1, 8, width), lambda b, s, pt: (b, 0, 0)),
            pl.BlockSpec((1, 8, width), lambda b, s, pt: (b, 0, 0)),
            pl.BlockSpec((page, page), lambda b, s, pt: (0, 0)),
            pl.BlockSpec((grp * page, grp * page), lambda b, s, pt: (0, 0)),
        ] + [page_spec(u) for u in range(ppb)] + [page_spec(u) for u in range(ppb)],
        out_specs=pl.BlockSpec((1, seq, width), lambda b, s, pt: (b, 0, 0)),
        scratch_shapes=[pltpu.VMEM((nq, width), F32), pltpu.VMEM((1, nq), F32)]
        + [pltpu.VMEM((grp * page, width), BF16)] * (2 * (ppb // grp)))
    return pl.pallas_call(
        kern, grid_spec=grid_spec,
        out_shape=jax.ShapeDtypeStruct((n_seq, seq, width), BF16),
        compiler_params=_cparams(("arbitrary", "arbitrary"), 52), name="sb_sample")(
            page_table, qbd, brow, pad(k_s), pad(v_s), tn, tp,
            *([cache_k] * ppb), *([cache_v] * ppb))


def _top16_rows(s):
    n = s.shape[0]
    rid = lax.broadcasted_iota(jnp.int32, s.shape, 0)
    out = []
    for _ in range(PEER_K):
        m = jnp.max(s, axis=0, keepdims=True)
        first = jnp.min(jnp.where(s == m, rid, n), axis=0, keepdims=True)
        out.append(m)
        s = jnp.where(rid == first, -jnp.inf, s)
    return out


def _peer_topk_kernel(q_ref, keys_ref, s1_ref, s2_ref, st_ref, *, n_heads):
    tm = q_ref.shape[0]
    for h in range(n_heads):
        tops = []
        for c in range(2):
            qh = q_ref[:, (2 * h + c) * 128:(2 * h + c + 1) * 128]
            sc = _dot_nt(keys_ref[h, c].astype(BF16), qh)
            (s1_ref, s2_ref)[c][h] = sc
            tops.append(_top16_rows(sc))
        sv2 = jnp.concatenate(tops[1], axis=0)
        cand = jnp.concatenate([tops[0][a] + sv2 for a in range(PEER_K)], axis=0)
        best = _top16_rows(cand)
        m = best[0]
        zsum = sum(jnp.exp(v - m) for v in best)
        max2 = tops[1][0]
        st_ref[h] = jnp.concatenate(
            [best[PEER_K - 1], m + jnp.log(zsum) - max2, max2, jnp.zeros((5, tm), F32)], axis=0)


def peer_topk(qp, sub_keys, tm):
    r = qp.shape[0]
    n_heads = sub_keys.shape[0]
    kern = functools.partial(_peer_topk_kernel, n_heads=n_heads)
    return pl.pallas_call(
        kern, grid=(r // tm,),
        in_specs=[pl.BlockSpec((tm, qp.shape[1]), lambda i: (i, 0)),
                  pl.BlockSpec(sub_keys.shape, lambda i: (0, 0, 0, 0))],
        out_specs=[pl.BlockSpec((n_heads, PEER_KEYS, tm), lambda i: (0, 0, i)),
                   pl.BlockSpec((n_heads, PEER_KEYS, tm), lambda i: (0, 0, i)),
                   pl.BlockSpec((n_heads, 8, tm), lambda i: (0, 0, i))],
        out_shape=[jax.ShapeDtypeStruct((n_heads, PEER_KEYS, r), F32),
                   jax.ShapeDtypeStruct((n_heads, PEER_KEYS, r), F32),
                   jax.ShapeDtypeStruct((n_heads, 8, r), F32)],
        compiler_params=_cparams(("parallel",), 40), name="peer_topk")(qp, sub_keys)


def _peer_dense_kernel(hn_ref, u_ref, vlo_ref, vhi_ref, s1_ref, s2_ref, st_ref, o_ref, p_prev, e2_s,
                       *, n_heads, half, n_trips):
    k = pl.program_id(1)

    @pl.when(k == 0)
    def _():
        o_ref[...] = jnp.zeros_like(o_ref)
        p_prev[...] = jnp.zeros_like(p_prev)
        for h in range(n_heads):
            e2_s[h] = jnp.exp(s2_ref[h] - st_ref[h, 2:3, :])

    live = (k < n_trips - 1).astype(F32)
    hn = hn_ref[...]

    hids = [_dot_nt(u_ref[part * half:(part + 1) * half, :], hn) for part in range(2)]
    first = _dot_tn(p_prev[...], vlo_ref[...])

    def gate_act(part):
        hid = hids[part]
        act = (0.5 * live) * hid * (1.0 + lax.erf(hid * (2.0 ** -0.5)))
        out = []
        for ii in range(half // PEER_KEYS):
            i_row = part * (half // PEER_KEYS) + ii
            wsum = None
            for h in range(n_heads):
                s1 = s1_ref[i_row, h:h + 1, :]
                sel = (s1 + s2_ref[h]) >= st_ref[h, 0:1, :]
                g = jnp.where(sel, jnp.exp(s1 - st_ref[h, 1:2, :]) * e2_s[h], 0.0)
                wsum = g if wsum is None else wsum + g
            out.append((wsum * act[ii * PEER_KEYS:(ii + 1) * PEER_KEYS, :]).astype(BF16))
        return jnp.concatenate(out, axis=0)

    pa = gate_act(0)
    second = _dot_tn(pa, vhi_ref[...])
    p_prev[...] = gate_act(1)
    o_ref[...] += first + second


def peer_dense(hn, eu, ev, s1, s2, st, tm, half):
    r, d = hn.shape
    n_exp = eu.shape[0]
    n_heads = s2.shape[0]
    n_pairs = n_exp // (2 * half)
    n_half = n_exp // half
    kern = functools.partial(_peer_dense_kernel, n_heads=n_heads, half=half, n_trips=n_pairs + 1)
    once = pl.Buffered(1)
    return pl.pallas_call(
        kern, grid=(pl.cdiv(r, tm), n_pairs + 1),
        in_specs=[pl.BlockSpec((tm, d), lambda i, k: (i, 0), pipeline_mode=once),
                  pl.BlockSpec((2 * half, d), lambda i, k: (jnp.minimum(k, n_pairs - 1), 0)),
                  pl.BlockSpec((half, d), lambda i, k: (jnp.maximum(2 * k - 1, 0), 0)),
                  pl.BlockSpec((half, d), lambda i, k: (jnp.minimum(2 * k, n_half - 1), 0)),
                  pl.BlockSpec((2 * half // PEER_KEYS, n_heads, tm),
                               lambda i, k: (jnp.minimum(k, n_pairs - 1), 0, i)),
                  pl.BlockSpec((n_heads, PEER_KEYS, tm), lambda i, k: (0, 0, i), pipeline_mode=once),
                  pl.BlockSpec((n_heads, 8, tm), lambda i, k: (0, 0, i), pipeline_mode=once)],
        out_specs=pl.BlockSpec((tm, d), lambda i, k: (i, 0), pipeline_mode=once),
        out_shape=jax.ShapeDtypeStruct((r, d), F32),
        scratch_shapes=[pltpu.VMEM((half, tm), BF16), pltpu.VMEM((n_heads, PEER_KEYS, tm), F32)],
        compiler_params=_cparams(("parallel", "arbitrary"), 56), name="peer_dense")(
            hn, eu, ev, ev, s1, s2, st)


def kernel(x_prompt, x_sample, cache_k, cache_v, state_ssm, state_conv, page_table, norm_mix, w_in,
           conv_w, conv_b, dt_bias, a_log, d_skip, ssm_norm_w, sb_bias, w_branch_a, w_branch_b,
           w_out, norm_ffn, w_query, sub_keys, expert_u, expert_v, norm_final):
    depth = w_in.shape[0]
    assert depth == 1 and x_prompt.shape[0] == 1
    _, t, d = x_prompt.shape
    n_seq, seq, _ = x_sample.shape
    n_tok = n_seq * seq
    r = t + n_tok
    n_ssm_heads = dt_bias.shape[1]
    d_inner = n_ssm_heads * HEAD_P
    g = n_ssm_heads // GROUP_HEADS
    conv_dim = conv_w.shape[2]
    n_sb = sb_bias.shape[1]
    sbw = n_sb * SB_DH
    assert t % 256 == 0 and n_tok == 128 and r % 640 == 0

    wl = w_in[0]
    dt0 = d_inner + conv_dim
    w_rest = wl[:, dt0 + n_ssm_heads:].astype(BF16)
    w_dt = jnp.pad(wl[:, dt0:dt0 + n_ssm_heads], ((0, 0), (0, 128 - n_ssm_heads))).astype(BF16)

    x_all = jnp.concatenate([x_prompt[0], x_sample.reshape(n_tok, d)], axis=0)
    xn = rmsnorm(x_all, norm_mix[0], BF16)
    tm5 = r // 5
    (proj,) = matmul(xn, wl, [F32], tm5, 256, col0=0, n=dt0)
    qscale = SB_DH ** -0.5 * LOG2E
    tm_p = max(c for c in range(128, 1025, 128) if t % c == 0)
    prm = dict(tm=tm_p, tn=512, row0=0, m=t)
    smp = dict(tm=n_tok, tn=512, row0=t, m=n_tok)
    (q_p,) = matmul(xn, w_rest, [BF16], col0=0, n=sbw, scale=qscale, **prm)
    (q_s,) = matmul(xn, w_rest, [BF16], col0=0, n=sbw, scale=qscale, **smp)
    kp_f, kp_bf = matmul(xn, w_rest, [F32, BF16], col0=sbw, n=sbw, **prm)
    (k_s,) = matmul(xn, w_rest, [F32], col0=sbw, n=sbw, **smp)
    vp_f, vp_bf = matmul(xn, w_rest, [F32, BF16], col0=2 * sbw, n=sbw, **prm)
    (v_s,) = matmul(xn, w_rest, [F32], col0=2 * sbw, n=sbw, **smp)
    (gates,) = matmul(xn, w_rest, [F32], tm5, 512, col0=3 * sbw, n=2 * d)
    dt_raw = matmul(xn, w_dt, [F32], tm5, 128)[0][:, :n_ssm_heads]

    def head_layouts(v):
        rows = v.shape[0]
        vg = v.reshape(rows, g, GROUP_HEADS).transpose(1, 0, 2)
        nat = jnp.pad(vg, ((0, 0), (0, 0), (0, 128 - GROUP_HEADS)))
        tr = jnp.pad(vg.transpose(0, 2, 1), ((0, 0), (0, 16 - GROUP_HEADS), (0, 0)))
        return nat, tr

    hp = jnp.stack([dt_bias[0], a_log[0], d_skip[0]], axis=0)
    hp_g = hp.reshape(3, g, GROUP_HEADS).transpose(1, 0, 2)
    hpn = jnp.pad(hp_g, ((0, 0), (0, 5), (0, 128 - GROUP_HEADS)))
    hpt = jnp.pad(hp_g.transpose(0, 2, 1), ((0, 0), (0, 16 - GROUP_HEADS), (0, 125)))
    nw = ssm_norm_w[0].reshape(1, d_inner)
    cw, cbias = conv_w[0], conv_b[0].reshape(1, conv_dim)

    dtn_p, dtt_p = head_layouts(dt_raw[:t])
    ya_head, ssm_p = ssd_prompt(proj, t, dtn_p, dtt_p, cw, cbias, hpn, hpt, nw, L=256)

    xbc_s = proj[t:, d_inner:d_inner + conv_dim].reshape(n_seq, seq, conv_dim)
    ext_s = jnp.concatenate([state_conv[0], xbc_s], axis=1)
    sh = jnp.stack([ext_s[:, k:k + seq].reshape(n_tok, conv_dim) for k in range(4)], axis=0)
    dtn_s, dtt_s = head_layouts(dt_raw[t:])
    ya, ssm_s = ssd_sample(sh, proj, t // n_tok, dtn_s, dtt_s, cw, cbias, hpn, hpt, nw,
                           state_ssm[0].reshape(n_seq, d_inner, D_STATE), seq, ya_head)

    yb_p = sb_prompt(q_p, kp_bf, vp_bf, t, sb_bias[0], blk=256, hb=8)
    n_phys, page = cache_k.shape[1], cache_k.shape[2]
    yb_s = sb_sample(q_s, k_s, v_s, cache_k.reshape(n_phys, page * n_sb, SB_DH),
                     cache_v.reshape(n_phys, page * n_sb, SB_DH), page_table, sb_bias[0], n_sb, seq)

    yb = jnp.concatenate([yb_p, yb_s.reshape(n_tok, sbw)], axis=0)
    merged = merge_branches(ya, yb, w_branch_a[0].astype(BF16), w_branch_b[0].astype(BF16),
                            gates, 0, d, tm=640, tn=512)
    h1 = matmul_resid(merged, w_out[0].astype(BF16), x_all, tm=640, tn=512)

    hn = rmsnorm(h1, norm_ffn[0], BF16)
    (qp,) = matmul(hn, w_query[0].astype(BF16), [BF16], tm5, 512)
    s1_hk, s2, stats = peer_topk(qp, sub_keys[0], tm=128)
    s1 = s1_hk.transpose(1, 0, 2)
    peer = peer_dense(hn, expert_u[0].astype(BF16), expert_v[0].astype(BF16), s1, s2, stats,
                      tm=768 if r > 768 else r, half=256)
    y_p, y_s = rmsnorm_add_split(h1, peer, norm_final, t)

    y_prompt = y_p.reshape(1, t, d)
    y_sample = y_s.reshape(n_seq, seq, d)
    k_prompt = kp_f.reshape(1, 1, t, n_sb, SB_DH)
    v_prompt = vp_f.reshape(1, 1, t, n_sb, SB_DH)
    ssm_prompt = ssm_p.reshape(1, 1, n_ssm_heads, HEAD_P, D_STATE)
    conv_prompt = proj[t - 3:t, d_inner:d_inner + conv_dim].reshape(1, 1, 3, conv_dim)
    k_sample = k_s.reshape(1, n_seq, seq, n_sb, SB_DH)
    v_sample = v_s.reshape(1, n_seq, seq, n_sb, SB_DH)
    ssm_sample = ssm_s.reshape(1, n_seq, n_ssm_heads, HEAD_P, D_STATE)
    conv_sample = ext_s[:, seq:].reshape(1, n_seq, 3, conv_dim)
    return (y_prompt, y_sample, k_prompt, v_prompt, ssm_prompt, conv_prompt,
            k_sample, v_sample, ssm_sample, conv_sample)
```

```python
import functools
import math

import numpy as np
import jax
import jax.numpy as jnp
from jax import lax
from jax.experimental import pallas as pl
from jax.experimental.pallas import tpu as pltpu

F32 = jnp.float32
BF16 = jnp.bfloat16
EPS = 1e-6
NEG_BIG = -1e30

HEAD_P = 64
GROUP_HEADS = 8
GROUP_W = HEAD_P * GROUP_HEADS
D_STATE = 128
SB_DH = 128
PEER_K = 16
PEER_KEYS = 128


def _cparams(sem, vmem_mb):
    return pltpu.CompilerParams(dimension_semantics=sem,
                                vmem_limit_bytes=vmem_mb * 1024 * 1024)


def _softplus(x):
    return jnp.maximum(x, 0.0) + jnp.log1p(jnp.exp(-jnp.abs(x)))


LOG2E = 1.4426950408889634


def _softplus2(x):
    return jnp.where(x > 64.0, x, jnp.log(1.0 + jnp.exp2(x)) * LOG2E)


def _silu(x):
    return x * jax.nn.sigmoid(x)


def _split2(x):
    hi = x.astype(BF16)
    lo = (x - hi.astype(F32)).astype(BF16)
    return hi, lo


def _split3(x):
    hi = x.astype(BF16)
    r = x - hi.astype(F32)
    mid = r.astype(BF16)
    lo = (r - mid.astype(F32)).astype(BF16)
    return hi, mid, lo


def _dot(a, b):
    return jnp.dot(a, b, preferred_element_type=F32)


def _dot_nt(a, b):
    return lax.dot_general(a, b, (((1,), (1,)), ((), ())), preferred_element_type=F32)


def _dot_tn(a, b):
    return lax.dot_general(a, b, (((0,), (0,)), ((), ())), preferred_element_type=F32)


def _rms_kernel(x_ref, w_ref, o_ref):
    x = x_ref[...]
    r = x * lax.rsqrt(jnp.mean(x * x, axis=-1, keepdims=True) + EPS)
    o_ref[...] = (r * w_ref[...]).astype(o_ref.dtype)


def rmsnorm(x, w, out_dtype, tm=128):
    m, d = x.shape
    row = pl.BlockSpec((tm, d), lambda i: (i, 0))
    return pl.pallas_call(
        _rms_kernel, grid=(m // tm,),
        in_specs=[row, pl.BlockSpec((1, d), lambda i: (0, 0))], out_specs=row,
        out_shape=jax.ShapeDtypeStruct((m, d), out_dtype),
        compiler_params=_cparams(("parallel",), 40), name="rmsnorm")(x, w.reshape(1, d))


def _rms_add_split_kernel(x_ref, y_ref, w_ref, head_ref, tail_ref, *, head_blocks):
    i = pl.program_id(0)
    x = x_ref[...] + y_ref[...]
    r = x * lax.rsqrt(jnp.mean(x * x, axis=-1, keepdims=True) + EPS)
    out = r * w_ref[...]

    @pl.when(i < head_blocks)
    def _():
        head_ref[...] = out

    @pl.when(i >= head_blocks)
    def _():
        tail_ref[...] = out


def rmsnorm_add_split(x, y, w, n_head, tm=128):
    m, d = x.shape
    hb = n_head // tm
    row = pl.BlockSpec((tm, d), lambda i: (i, 0))
    return pl.pallas_call(
        functools.partial(_rms_add_split_kernel, head_blocks=hb), grid=(m // tm,),
        in_specs=[row, row, pl.BlockSpec((1, d), lambda i: (0, 0))],
        out_specs=[pl.BlockSpec((tm, d), lambda i: (jnp.minimum(i, hb - 1), 0)),
                   pl.BlockSpec((tm, d), lambda i: (jnp.maximum(i - hb, 0), 0))],
        out_shape=[jax.ShapeDtypeStruct((n_head, d), F32),
                   jax.ShapeDtypeStruct((m - n_head, d), F32)],
        compiler_params=_cparams(("arbitrary",), 40), name="rmsnorm_out")(x, y, w.reshape(1, d))


def _mm_kernel(a_ref, w_ref, *o_refs, scale):
    acc = _dot(a_ref[...], w_ref[...].astype(BF16))
    if scale is not None:
        acc = acc * scale
    for o_ref in o_refs:
        o_ref[...] = acc.astype(o_ref.dtype)


def matmul(a, w, out_dtypes, tm, tn, col0=0, n=None, scale=None, row0=0, m=None):
    k = a.shape[1]
    m = a.shape[0] if m is None else m
    n = w.shape[1] if n is None else n
    c0, r0 = col0 // tn, row0 // tm
    out = pl.BlockSpec((tm, tn), lambda i, j: (i, j))
    return pl.pallas_call(
        functools.partial(_mm_kernel, scale=scale), grid=(m // tm, n // tn),
        in_specs=[pl.BlockSpec((tm, k), lambda i, j: (r0 + i, 0)),
                  pl.BlockSpec((k, tn), lambda i, j: (0, c0 + j))],
        out_specs=[out] * len(out_dtypes),
        out_shape=[jax.ShapeDtypeStruct((m, n), dt) for dt in out_dtypes],
        compiler_params=_cparams(("parallel", "arbitrary"), 52), name="matmul")(a, w)


def _mm_resid_kernel(a_ref, w_ref, x_ref, o_ref):
    o_ref[...] = x_ref[...] + _dot(a_ref[...], w_ref[...])


def matmul_resid(a, w, x, tm, tn):
    m, k = a.shape
    n = w.shape[1]
    return pl.pallas_call(
        _mm_resid_kernel, grid=(m // tm, n // tn),
        in_specs=[pl.BlockSpec((tm, k), lambda i, j: (i, 0)),
                  pl.BlockSpec((k, tn), lambda i, j: (0, j)),
                  pl.BlockSpec((tm, tn), lambda i, j: (i, j))],
        out_specs=pl.BlockSpec((tm, tn), lambda i, j: (i, j)),
        out_shape=jax.ShapeDtypeStruct((m, n), F32),
        compiler_params=_cparams(("parallel", "arbitrary"), 52), name="matmul_resid")(a, w, x)


def _realign_kernel(a_ref, b_ref, o_ref, *, shift):
    x = jnp.concatenate([a_ref[...], b_ref[...]], axis=1)
    o_ref[...] = x[:, shift:shift + o_ref.shape[1]].astype(o_ref.dtype)


def realign_cast(w, col0, tr=512, tc=2048):
    k, total = w.shape
    n = total - col0
    base = (col0 // 128) * 128
    assert base % tc == 0 and n % tc == 0 and k % tr == 0
    b0 = base // tc
    return pl.pallas_call(
        functools.partial(_realign_kernel, shift=col0 - base), grid=(k // tr, n // tc),
        in_specs=[pl.BlockSpec((tr, tc), lambda i, j: (i, b0 + j)),
                  pl.BlockSpec((tr, 128), lambda i, j: (i, (b0 + j + 1) * (tc // 128)))],
        out_specs=pl.BlockSpec((tr, tc), lambda i, j: (i, j)),
        out_shape=jax.ShapeDtypeStruct((k, n), BF16),
        compiler_params=_cparams(("parallel", "parallel"), 40), name="realign_cast")(w, w)


def _mm_merge_kernel(ya_ref, yb_ref, wa_ref, wb_ref, ga_ref, gb_ref, o_ref):
    a = _dot(ya_ref[...], wa_ref[...])
    b = _dot(yb_ref[...], wb_ref[...])
    o_ref[...] = (jax.nn.sigmoid(ga_ref[...]) * a + jax.nn.sigmoid(gb_ref[...]) * b).astype(o_ref.dtype)


def merge_branches(ya, yb, wa, wb, proj, ga_col, gb_col, tm, tn):
    m, ka = ya.shape
    kb = yb.shape[1]
    n = wa.shape[1]
    ga0, gb0 = ga_col // tn, gb_col // tn
    return pl.pallas_call(
        _mm_merge_kernel, grid=(m // tm, n // tn),
        in_specs=[pl.BlockSpec((tm, ka), lambda i, j: (i, 0)),
                  pl.BlockSpec((tm, kb), lambda i, j: (i, 0)),
                  pl.BlockSpec((ka, tn), lambda i, j: (0, j)),
                  pl.BlockSpec((kb, tn), lambda i, j: (0, j)),
                  pl.BlockSpec((tm, tn), lambda i, j: (i, ga0 + j)),
                  pl.BlockSpec((tm, tn), lambda i, j: (i, gb0 + j))],
        out_specs=pl.BlockSpec((tm, tn), lambda i, j: (i, j)),
        out_shape=jax.ShapeDtypeStruct((m, n), BF16),
        compiler_params=_cparams(("parallel", "arbitrary"), 52), name="merge")(
            ya, yb, wa, wb, proj, proj)


def _conv_silu(ext, w_ref, b_ref, L):
    acc = b_ref[...] + w_ref[3:4, :] * ext[8:8 + L, :]
    acc = acc + w_ref[2:3, :] * ext[7:7 + L, :]
    acc = acc + w_ref[1:2, :] * ext[6:6 + L, :]
    acc = acc + w_ref[0:1, :] * ext[5:5 + L, :]
    return _silu(acc)


def _pair_terms(pr, x, dt, acs, acs_t, last_t, cb, ch, mask, hpn, lo_half):
    ra, rb = 2 * pr, 2 * pr + 1
    cola, colb = acs[:, ra:ra + 1], acs[:, rb:rb + 1]
    rowa, rowb = acs_t[ra:ra + 1, :], acs_t[rb:rb + 1, :]
    ma = cb * jnp.exp(jnp.where(mask, cola - rowa, NEG_BIG))
    mb = cb * jnp.exp(jnp.where(mask, colb - rowb, NEG_BIG))
    mcat = jnp.concatenate([ma.astype(BF16), mb.astype(BF16)], axis=1)
    xp = x[:, pr * 128:(pr + 1) * 128]
    xdt = xp * jnp.where(lo_half, dt[:, ra:ra + 1], dt[:, rb:rb + 1])
    rhs = jnp.concatenate([jnp.where(lo_half, xdt, 0.0), jnp.where(lo_half, 0.0, xdt)],
                          axis=0).astype(BF16)
    yd = _dot(mcat, rhs)
    eap = jnp.where(lo_half, jnp.exp(cola), jnp.exp(colb))
    dsk = jnp.where(lo_half[0:1, :], hpn[2:3, ra:ra + 1], hpn[2:3, rb:rb + 1])
    y = yd + ch[:, pr * 128:(pr + 1) * 128] * eap + xp * dsk
    return y, xdt, eap


def _ssd_prompt_kernel(xs_ref, b_ref, c_ref, z_ref, dtn_ref, dtt_ref,
                       cwx_ref, cwb_ref, cwc_ref, cbx_ref, cbb_ref, cbc_ref,
                       hpn_ref, hpt_ref, nw_ref, lin_ref, uin_ref,
                       y_ref, hout_ref, extx, extb, extc, hs, *, L):
    c = pl.program_id(1)

    @pl.when(c == 0)
    def _():
        extx[0:8, :] = jnp.zeros((8, GROUP_W), F32)
        extb[0:8, :] = jnp.zeros((8, D_STATE), F32)
        extc[0:8, :] = jnp.zeros((8, D_STATE), F32)
        hs[...] = jnp.zeros_like(hs)

    extx[8:8 + L, :] = xs_ref[...]
    extb[8:8 + L, :] = b_ref[...]
    extc[8:8 + L, :] = c_ref[...]
    x = _conv_silu(extx, cwx_ref, cbx_ref, L)
    bm = _conv_silu(extb, cwb_ref, cbb_ref, L).astype(BF16)
    cm = _conv_silu(extc, cwc_ref, cbc_ref, L).astype(BF16)
    extx[0:8, :] = extx[L:L + 8, :]
    extb[0:8, :] = extb[L:L + 8, :]
    extc[0:8, :] = extc[L:L + 8, :]

    hpn = hpn_ref[0]
    hpt = hpt_ref[0]
    dt = _softplus(dtn_ref[0] + hpn[0:1, :])
    d_a = dt * (-jnp.exp(hpn[1:2, :]))
    d_at = _softplus(dtt_ref[0] + hpt[:, 0:1]) * (-jnp.exp(hpt[:, 1:2]))
    lin = lin_ref[...]
    uin = uin_ref[...]
    acs = sum(_dot(lin, p) for p in _split3(d_a))
    acs_t = sum(_dot(p, uin) for p in _split3(d_at))

    rows = lax.broadcasted_iota(jnp.int32, (L, L), 0)
    cols = lax.broadcasted_iota(jnp.int32, (L, L), 1)
    causal = rows >= cols
    lo_half = lax.broadcasted_iota(jnp.int32, (L, 128), 1) < HEAD_P
    sub_lo = lax.broadcasted_iota(jnp.int32, (128, D_STATE), 0) < HEAD_P

    cb = _dot_nt(cm, bm)
    ch = _dot_nt(cm, hs[...].astype(BF16))

    ys = []
    for pr in range(GROUP_HEADS // 2):
        ra, rb = 2 * pr, 2 * pr + 1
        y, xdt, _ = _pair_terms(pr, x, dt, acs, acs_t, None, cb, ch, causal, hpn, lo_half)
        ys.append(y)
        lasta = acs_t[ra:ra + 1, L - 1:L]
        lastb = acs_t[rb:rb + 1, L - 1:L]
        dend = jnp.where(lo_half, jnp.exp(lasta - acs[:, ra:ra + 1]),
                         jnp.exp(lastb - acs[:, rb:rb + 1]))
        s_new = _dot_tn((xdt * dend).astype(BF16), bm)
        cd = jnp.where(sub_lo, jnp.exp(lasta), jnp.exp(lastb))
        sl = slice(pr * 128, (pr + 1) * 128)
        hs[sl, :] = hs[sl, :] * cd + s_new

    yg = jnp.concatenate(ys, axis=1)
    u = yg * _silu(z_ref[...])
    u = u * lax.rsqrt(jnp.mean(u * u, axis=-1, keepdims=True) + EPS)
    y_ref[...] = (u * nw_ref[...]).astype(y_ref.dtype)

    @pl.when(c == pl.num_programs(1) - 1)
    def _():
        hout_ref[...] = hs[...]


def _tri_incl(n):
    return np.tril(np.ones((n, n), np.float32))


def ssd_prompt(proj, t, dt_n, dt_t, conv_w, conv_b, hpn, hpt, norm_w, L):
    g = dt_n.shape[0]
    nc = t // L
    d_inner = g * GROUP_W
    xb = d_inner // GROUP_W
    bb = (2 * d_inner) // D_STATE
    cbk = bb + g
    lin = jnp.asarray(_tri_incl(L), BF16)
    uin = jnp.asarray(_tri_incl(L).T, BF16)
    const = lambda gi, ci: (0, 0)
    kern = functools.partial(_ssd_prompt_kernel, L=L)
    return pl.pallas_call(
        kern, grid=(g, nc),
        in_specs=[
            pl.BlockSpec((L, GROUP_W), lambda gi, ci: (ci, xb + gi)),
            pl.BlockSpec((L, D_STATE), lambda gi, ci: (ci, bb + gi)),
            pl.BlockSpec((L, D_STATE), lambda gi, ci: (ci, cbk + gi)),
            pl.BlockSpec((L, GROUP_W), lambda gi, ci: (ci, gi)),
            pl.BlockSpec((1, L, 128), lambda gi, ci: (gi, ci, 0)),
            pl.BlockSpec((1, 16, L), lambda gi, ci: (gi, 0, ci)),
            pl.BlockSpec((4, GROUP_W), lambda gi, ci: (0, gi)),
            pl.BlockSpec((4, D_STATE), lambda gi, ci: (0, bb - xb * 4 + gi)),
            pl.BlockSpec((4, D_STATE), lambda gi, ci: (0, bb - xb * 4 + g + gi)),
            pl.BlockSpec((1, GROUP_W), lambda gi, ci: (0, gi)),
            pl.BlockSpec((1, D_STATE), lambda gi, ci: (0, bb - xb * 4 + gi)),
            pl.BlockSpec((1, D_STATE), lambda gi, ci: (0, bb - xb * 4 + g + gi)),
            pl.BlockSpec((1, 8, 128), lambda gi, ci: (gi, 0, 0)),
            pl.BlockSpec((1, 16, 128), lambda gi, ci: (gi, 0, 0)),
            pl.BlockSpec((1, GROUP_W), lambda gi, ci: (0, gi)),
            pl.BlockSpec((L, L), const),
            pl.BlockSpec((L, L), const),
        ],
        out_specs=[pl.BlockSpec((L, GROUP_W), lambda gi, ci: (ci, gi)),
                   pl.BlockSpec((GROUP_W, D_STATE), lambda gi, ci: (gi, 0))],
        out_shape=[jax.ShapeDtypeStruct((proj.shape[0], d_inner), BF16),
                   jax.ShapeDtypeStruct((d_inner, D_STATE), F32)],
        scratch_shapes=[pltpu.VMEM((L + 8, GROUP_W), F32), pltpu.VMEM((L + 8, D_STATE), F32),
                        pltpu.VMEM((L + 8, D_STATE), F32), pltpu.VMEM((GROUP_W, D_STATE), F32)],
        compiler_params=_cparams(("arbitrary", "arbitrary"), 40), name="ssd_prompt")(
            proj, proj, proj, proj, dt_n, dt_t, conv_w, conv_w, conv_w, conv_b, conv_b, conv_b,
            hpn, hpt, norm_w, lin, uin)


def _ssd_sample_kernel(shx_ref, shb_ref, shc_ref, z_ref, dtn_ref, dtt_ref,
                       cwx_ref, cwb_ref, cwc_ref, cbx_ref, cbb_ref, cbc_ref,
                       hpn_ref, hpt_ref, nw_ref, lseg_ref, useg_ref, sseg_ref, st_ref, y_all_ref,
                       y_ref, hout_ref, yacc, cm_s, bm_s, ea_s, xwt_s, cdl_s, *, n_tok, seq):
    b = pl.program_id(1)

    @pl.when(b == 0)
    def _():
        def conv(sh_ref, w_ref, b_ref):
            acc = b_ref[...] + w_ref[0:1, :] * sh_ref[0]
            for k in range(1, 4):
                acc = acc + w_ref[k:k + 1, :] * sh_ref[k]
            return _silu(acc)

        x = conv(shx_ref, cwx_ref, cbx_ref)
        bm = conv(shb_ref, cwb_ref, cbb_ref).astype(BF16)
        cm = conv(shc_ref, cwc_ref, cbc_ref).astype(BF16)
        hpn = hpn_ref[0]
        hpt = hpt_ref[0]
        dt = _softplus(dtn_ref[0] + hpn[0:1, :])
        d_a = dt * (-jnp.exp(hpn[1:2, :]))
        d_at = _softplus(dtt_ref[0] + hpt[:, 0:1]) * (-jnp.exp(hpt[:, 1:2]))
        lseg = lseg_ref[...]
        useg = useg_ref[...]
        sseg = sseg_ref[...]
        pa, pat = _split3(d_a), _split3(d_at)
        acs = sum(_dot(lseg, p) for p in pa)
        tot = sum(_dot(sseg, p) for p in pa)
        acs_t = sum(_dot(p, useg) for p in pat)
        tot_t = sum(_dot(p, sseg) for p in pat)
        mask = lseg > 0
        lo_half = lax.broadcasted_iota(jnp.int32, (n_tok, 128), 1) < HEAD_P
        cb = _dot_nt(cm, bm)
        zero_ch = jnp.zeros((n_tok, GROUP_W), F32)
        ys, xws, eas = [], [], []
        for pr in range(GROUP_HEADS // 2):
            ra, rb = 2 * pr, 2 * pr + 1
            y, xdt, eap = _pair_terms(pr, x, dt, acs, acs_t, None, cb, zero_ch, mask, hpn, lo_half)
            dend = jnp.where(lo_half, jnp.exp(tot[:, ra:ra + 1] - acs[:, ra:ra + 1]),
                             jnp.exp(tot[:, rb:rb + 1] - acs[:, rb:rb + 1]))
            ys.append(y)
            xws.append(xdt * dend)
            eas.append(eap)
        yacc[...] = jnp.concatenate(ys, axis=1)
        ea_s[...] = jnp.concatenate(eas, axis=1)
        xwt_s[...] = jnp.transpose(jnp.concatenate(xws, axis=1)).astype(BF16)
        cm_s[...] = cm
        bm_s[...] = bm
        cdl_s[...] = jnp.concatenate(
            [jnp.broadcast_to(tot_t[r:r + 1, :], (HEAD_P, n_tok)) for r in range(GROUP_HEADS)], axis=0)

    h0 = st_ref[0]
    lo = b * seq
    rid = lax.broadcasted_iota(jnp.int32, (n_tok, GROUP_W), 0)
    in_rows = jnp.abs(2 * (rid - lo) - (seq - 1)) < seq
    lid = lax.broadcasted_iota(jnp.int32, (GROUP_W, n_tok), 1)
    in_lanes = jnp.abs(2 * (lid - lo) - (seq - 1)) < seq
    ch = _dot_nt(cm_s[...], h0.astype(BF16))
    yacc[...] += jnp.where(in_rows, ch * ea_s[...], 0.0)
    s_new = _dot(jnp.where(in_lanes, xwt_s[...], jnp.zeros_like(xwt_s[...])), bm_s[...])
    cdcol = jnp.sum(jnp.where(lid == lo, cdl_s[...], 0.0), axis=-1, keepdims=True)
    hout_ref[0] = h0 * jnp.exp(cdcol) + s_new

    @pl.when(b == pl.num_programs(1) - 1)
    def _():
        u = yacc[...] * _silu(z_ref[...])
        u = u * lax.rsqrt(jnp.mean(u * u, axis=-1, keepdims=True) + EPS)
        y_ref[...] = (u * nw_ref[...]).astype(y_ref.dtype)


def ssd_sample(sh, proj, z_row_block, dt_n, dt_t, conv_w, conv_b, hpn, hpt, norm_w, state, seq,
               y_all):
    g = dt_n.shape[0]
    n_tok = sh.shape[1]
    n_seq = n_tok // seq
    d_inner = g * GROUP_W
    xb4 = d_inner // D_STATE
    sid = np.arange(n_tok) // seq
    same = (sid[:, None] == sid[None, :]).astype(np.float32)
    lseg = jnp.asarray(same * _tri_incl(n_tok), BF16)
    useg = jnp.asarray((same * _tri_incl(n_tok)).T, BF16)
    sseg = jnp.asarray(same, BF16)
    const = lambda gi, bi: (0, 0)
    kern = functools.partial(_ssd_sample_kernel, n_tok=n_tok, seq=seq)
    return pl.pallas_call(
        kern, grid=(g, n_seq),
        in_specs=[
            pl.BlockSpec((4, n_tok, GROUP_W), lambda gi, bi: (0, 0, gi)),
            pl.BlockSpec((4, n_tok, D_STATE), lambda gi, bi: (0, 0, xb4 + gi)),
            pl.BlockSpec((4, n_tok, D_STATE), lambda gi, bi: (0, 0, xb4 + g + gi)),
            pl.BlockSpec((n_tok, GROUP_W), lambda gi, bi: (z_row_block, gi)),
            pl.BlockSpec((1, n_tok, 128), lambda gi, bi: (gi, 0, 0)),
            pl.BlockSpec((1, 16, n_tok), lambda gi, bi: (gi, 0, 0)),
            pl.BlockSpec((4, GROUP_W), lambda gi, bi: (0, gi)),
            pl.BlockSpec((4, D_STATE), lambda gi, bi: (0, xb4 + gi)),
            pl.BlockSpec((4, D_STATE), lambda gi, bi: (0, xb4 + g + gi)),
            pl.BlockSpec((1, GROUP_W), lambda gi, bi: (0, gi)),
            pl.BlockSpec((1, D_STATE), lambda gi, bi: (0, xb4 + gi)),
            pl.BlockSpec((1, D_STATE), lambda gi, bi: (0, xb4 + g + gi)),
            pl.BlockSpec((1, 8, 128), lambda gi, bi: (gi, 0, 0)),
            pl.BlockSpec((1, 16, 128), lambda gi, bi: (gi, 0, 0)),
            pl.BlockSpec((1, GROUP_W), lambda gi, bi: (0, gi)),
            pl.BlockSpec((n_tok, n_tok), const),
            pl.BlockSpec((n_tok, n_tok), const),
            pl.BlockSpec((n_tok, n_tok), const),
            pl.BlockSpec((1, GROUP_W, D_STATE), lambda gi, bi: (bi, gi, 0)),
            pl.BlockSpec(memory_space=pl.ANY),
        ],
        out_specs=[pl.BlockSpec((n_tok, GROUP_W), lambda gi, bi: (z_row_block, gi)),
                   pl.BlockSpec((1, GROUP_W, D_STATE), lambda gi, bi: (bi, gi, 0))],
        out_shape=[jax.ShapeDtypeStruct(y_all.shape, BF16),
                   jax.ShapeDtypeStruct(state.shape, F32)],
        input_output_aliases={19: 0},
        scratch_shapes=[pltpu.VMEM((n_tok, GROUP_W), F32), pltpu.VMEM((n_tok, D_STATE), BF16),
                        pltpu.VMEM((n_tok, D_STATE), BF16), pltpu.VMEM((n_tok, GROUP_W), F32),
                        pltpu.VMEM((GROUP_W, n_tok), BF16), pltpu.VMEM((GROUP_W, n_tok), F32)],
        compiler_params=_cparams(("arbitrary", "arbitrary"), 40), name="ssd_sample")(
            sh, sh, sh, proj, dt_n, dt_t, conv_w, conv_w, conv_w, conv_b, conv_b, conv_b,
            hpn, hpt, norm_w, lseg, useg, sseg, state, y_all)


def _sb_prompt_kernel(bias_ref, q_ref, k_ref, v_ref, u_ref, o_ref, *, blk, hb):
    g = pl.program_id(0)
    i = pl.program_id(1)
    umat = u_ref[...]
    rows = lax.broadcasted_iota(jnp.int32, (blk, blk), 0)
    cols = lax.broadcasted_iota(jnp.int32, (blk, blk), 1)
    strict = cols < rows

    def block(j, carry, masked):
        start = pl.multiple_of(j * blk, blk)
        heads = range(hb)
        hsl = [slice(h * SB_DH, (h + 1) * SB_DH) for h in heads]
        zs = [_dot_nt(q_ref[:, hsl[h]], k_ref[pl.ds(start, blk), hsl[h]])
              + bias_ref[g * hb + h] * LOG2E for h in heads]
        sps = [_softplus2(z) for z in zs]
        if masked:
            sps = [jnp.where(strict, sp, 0.0) for sp in sps]
        sufs = [_dot(sp.astype(BF16), umat) for sp in sps]
        ws = [jnp.exp2(zs[h] - sps[h] - sufs[h] - carry[h][1]) for h in heads]
        if masked:
            ws = [jnp.where(strict, w, 0.0) for w in ws]
        accs = [carry[h][0] + _dot(ws[h].astype(BF16), v_ref[pl.ds(start, blk), hsl[h]])
                for h in heads]
        runs = [carry[h][1] + jnp.sum(sps[h], axis=-1, keepdims=True) for h in heads]
        return tuple(zip(accs, runs))

    carry = tuple((jnp.zeros((blk, SB_DH), F32), jnp.zeros((blk, 1), F32)) for _ in range(hb))
    carry = block(i, carry, True)
    carry = lax.fori_loop(0, i, lambda t, c: block(i - 1 - t, c, False), carry)
    for h in range(hb):
        o_ref[:, h * SB_DH:(h + 1) * SB_DH] = carry[h][0].astype(o_ref.dtype)


def _tri_later(n):
    return np.tril(np.ones((n, n), np.float32), -1)


def sb_prompt(q, k, v, t, bias, blk, hb):
    width = q.shape[1]
    gw = hb * SB_DH
    umat = jnp.asarray(_tri_later(blk), BF16)
    kern = functools.partial(_sb_prompt_kernel, blk=blk, hb=hb)
    return pl.pallas_call(
        kern, grid=(width // gw, t // blk),
        in_specs=[
            pl.BlockSpec(memory_space=pltpu.SMEM),
            pl.BlockSpec((blk, gw), lambda g, i: (i, g)),
            pl.BlockSpec((t, gw), lambda g, i: (0, g), pipeline_mode=pl.Buffered(1)),
            pl.BlockSpec((t, gw), lambda g, i: (0, g), pipeline_mode=pl.Buffered(1)),
            pl.BlockSpec((blk, blk), lambda g, i: (0, 0)),
        ],
        out_specs=pl.BlockSpec((blk, gw), lambda g, i: (i, g)),
        out_shape=jax.ShapeDtypeStruct((t, width), BF16),
        compiler_params=_cparams(("arbitrary", "arbitrary"), 48), name="sb_prompt")(
            bias, q, k, v, umat)


def _sb_sample_kernel(pt_ref, qbd_ref, brow_ref, kn_ref, vn_ref, tn_ref, tp_ref, *rest,
                      n_heads, seq, page, ppb, grp):
    k_pages, v_pages = rest[:ppb], rest[ppb:2 * ppb]
    o_ref, acc, run = rest[2 * ppb:2 * ppb + 3]
    bufs = rest[2 * ppb + 3:]
    s = pl.program_id(1)
    qbd = qbd_ref[0]
    brow = brow_ref[...]
    nq = n_heads * 8

    def process(kb, vb, tmat, mask):
        z = _dot(kb, qbd) + brow
        sp = _softplus2(z)
        if mask is not None:
            sp = jnp.where(mask, sp, 0.0)
        hi, lo = _split2(sp)
        suffix = _dot(tmat, hi) + _dot(tmat, lo)
        w = jnp.exp2(z - sp - suffix - run[...])
        if mask is not None:
            w = jnp.where(mask, w, 0.0)
        acc[...] += _dot_tn(w.astype(BF16), vb)
        run[...] += jnp.sum(sp, axis=0, keepdims=True)

    @pl.when(s == 0)
    def _():
        acc[...] = jnp.zeros_like(acc)
        run[...] = jnp.zeros_like(run)
        fill = jnp.zeros((page - 8, n_heads * SB_DH), F32)
        knew = jnp.concatenate([kn_ref[0], fill], axis=0).astype(BF16)
        vnew = jnp.concatenate([vn_ref[0], fill], axis=0).astype(BF16)
        key = lax.broadcasted_iota(jnp.int32, (page, nq), 0)
        qpos = lax.broadcasted_iota(jnp.int32, (page, nq), 1) % 8
        process(knew, vnew, tn_ref[...], key < qpos)

    for gi in range(ppb // grp):
        kc, vc = bufs[2 * gi], bufs[2 * gi + 1]
        for u in range(grp):
            kp, vp = k_pages[gi * grp + u], v_pages[gi * grp + u]
            r0 = (grp - 1 - u) * page
            for h in range(n_heads):
                hs = slice(h * SB_DH, (h + 1) * SB_DH)
                kc[r0:r0 + page, hs] = kp[0, pl.ds(h, page, stride=n_heads), :].astype(BF16)
                vc[r0:r0 + page, hs] = vp[0, pl.ds(h, page, stride=n_heads), :].astype(BF16)
        process(kc[...], vc[...], tp_ref[...], None)

    @pl.when(s == pl.num_programs(1) - 1)
    def _():
        for h in range(n_heads):
            o_ref[0, :, h * SB_DH:(h + 1) * SB_DH] = (
                acc[h * 8:h * 8 + seq, h * SB_DH:(h + 1) * SB_DH].astype(o_ref.dtype))


def sb_sample(q_s, k_s, v_s, cache_k, cache_v, page_table, bias, n_heads, seq):
    n_seq, n_pages = page_table.shape
    page = cache_k.shape[1] // n_heads
    width = n_heads * SB_DH
    nq = n_heads * 8
    grp = 4
    ppb = 8 if n_pages % 8 == 0 else 4
    steps = n_pages // ppb
    q4 = jnp.pad(q_s.reshape(n_seq, seq, n_heads, SB_DH), ((0, 0), (0, 8 - seq), (0, 0), (0, 0)))
    eye = jnp.eye(n_heads, dtype=q_s.dtype)
    qbd = jnp.einsum("bthd,hg->bhdgt", q4, eye).reshape(n_seq, width, nq).astype(BF16)
    brow = jnp.repeat(bias * LOG2E, 8).reshape(1, nq)
    pad = lambda a: jnp.pad(a.reshape(n_seq, seq, width), ((0, 0), (0, 8 - seq), (0, 0)))
    tn = jnp.asarray(_tri_later(page).T, BF16)
    tp = jnp.asarray(_tri_later(grp * page).T, BF16)

    def page_spec(u):
        return pl.BlockSpec((1, page * n_heads, SB_DH),
                            lambda b, s, pt: (pt[b, n_pages - 1 - (ppb * s + u)], 0, 0))

    kern = functools.partial(_sb_sample_kernel, n_heads=n_heads, seq=seq, page=page, ppb=ppb,
                             grp=grp)
    grid_spec = pltpu.PrefetchScalarGridSpec(
        num_scalar_prefetch=1, grid=(n_seq, steps),
        in_specs=[
            pl.BlockSpec((1, width, nq), lambda b, s, pt: (b, 0, 0)),
            pl.BlockSpec((1, nq), lambda b, s, pt: (0, 0)),
            pl.BlockSpec((1, 8, width), lambda b, s, pt: (b, 0, 0)),
            pl.BlockSpec((1, 8, width), lambda b, s, pt: (b, 0, 0)),
            pl.BlockSpec((page, page), lambda b, s, pt: (0, 0)),
            pl.BlockSpec((grp * page, grp * page), lambda b, s, pt: (0, 0)),
        ] + [page_spec(u) for u in range(ppb)] + [page_spec(u) for u in range(ppb)],
        out_specs=pl.BlockSpec((1, seq, width), lambda b, s, pt: (b, 0, 0)),
        scratch_shapes=[pltpu.VMEM((nq, width), F32), pltpu.VMEM((1, nq), F32)]
        + [pltpu.VMEM((grp * page, width), BF16)] * (2 * (ppb // grp)))
    return pl.pallas_call(
        kern, grid_spec=grid_spec,
        out_shape=jax.ShapeDtypeStruct((n_seq, seq, width), BF16),
        compiler_params=_cparams(("arbitrary", "arbitrary"), 52), name="sb_sample")(
            page_table, qbd, brow, pad(k_s), pad(v_s), tn, tp,
            *([cache_k] * ppb), *([cache_v] * ppb))


def _top16_rows(s):
    n = s.shape[0]
    rid = lax.broadcasted_iota(jnp.int32, s.shape, 0)
    out = []
    for _ in range(PEER_K):
        m = jnp.max(s, axis=0, keepdims=True)
        first = jnp.min(jnp.where(s == m, rid, n), axis=0, keepdims=True)
        out.append(m)
        s = jnp.where(rid == first, -jnp.inf, s)
    return out


def _cmpx(v, i, j):
    a, b = v[i], v[j]
    v[i], v[j] = jnp.maximum(a, b), jnp.minimum(a, b)


def _bitonic_merge_desc(v):
    j = len(v) // 2
    while j >= 1:
        for i in range(len(v)):
            if i ^ j > i:
                _cmpx(v, i, i ^ j)
        j //= 2


def _bitonic_sort_desc(v):
    k = 2
    while k <= len(v):
        j = k // 2
        while j >= 1:
            for i in range(len(v)):
                if i ^ j > i:
                    _cmpx(v, *((i, i ^ j) if (i & k) == 0 else (i ^ j, i)))
            j //= 2
        k *= 2


def _top16_sorted(s):
    v = [s[8 * r:8 * r + 8, :] for r in range(PEER_K)]
    _bitonic_sort_desc(v)
    for shift in (4, 2, 1):
        w = [pltpu.roll(x, shift, 0) for x in v]
        v = [jnp.maximum(v[i], w[PEER_K - 1 - i]) for i in range(PEER_K)]
        _bitonic_merge_desc(v)
    return [x[0:1, :] for x in v]


def _peer_topk_kernel(q_ref, keys_ref, s1_ref, s2_ref, st_ref, *, n_heads):
    tm = q_ref.shape[0]
    for h in range(n_heads):
        tops = []
        for c in range(2):
            qh = q_ref[:, (2 * h + c) * 128:(2 * h + c + 1) * 128]
            sc = _dot_nt(keys_ref[h, c].astype(BF16), qh)
            (s1_ref, s2_ref)[c][h] = sc
            tops.append(_top16_sorted(sc))
        sv1 = jnp.concatenate(tops[0], axis=0)
        sv2 = jnp.concatenate(tops[1], axis=0)
        cand = jnp.concatenate(
            [tops[0][0] + sv2] + [tops[0][a] + sv2[0:8] for a in range(1, 8)]
            + [sv1[8:16] + tops[1][0]], axis=0)
        best = _top16_rows(cand)
        m = best[0]
        zsum = sum(jnp.exp(v - m) for v in best)
        max2 = tops[1][0]
        st_ref[h] = jnp.concatenate(
            [best[PEER_K - 1], m + jnp.log(zsum) - max2, max2, jnp.zeros((5, tm), F32)], axis=0)


def peer_topk(qp, sub_keys, tm):
    r = qp.shape[0]
    n_heads = sub_keys.shape[0]
    kern = functools.partial(_peer_topk_kernel, n_heads=n_heads)
    return pl.pallas_call(
        kern, grid=(r // tm,),
        in_specs=[pl.BlockSpec((tm, qp.shape[1]), lambda i: (i, 0)),
                  pl.BlockSpec(sub_keys.shape, lambda i: (0, 0, 0, 0))],
        out_specs=[pl.BlockSpec((n_heads, PEER_KEYS, tm), lambda i: (0, 0, i)),
                   pl.BlockSpec((n_heads, PEER_KEYS, tm), lambda i: (0, 0, i)),
                   pl.BlockSpec((n_heads, 8, tm), lambda i: (0, 0, i))],
        out_shape=[jax.ShapeDtypeStruct((n_heads, PEER_KEYS, r), F32),
                   jax.ShapeDtypeStruct((n_heads, PEER_KEYS, r), F32),
                   jax.ShapeDtypeStruct((n_heads, 8, r), F32)],
        compiler_params=_cparams(("parallel",), 40), name="peer_topk")(qp, sub_keys)


def _peer_dense_kernel(hn_ref, u_ref, vlo_ref, vhi_ref, s1_ref, s2_ref, st_ref, o_ref, p_prev, e2_s,
                       *, n_heads, half, n_trips):
    k = pl.program_id(1)

    @pl.when(k == 0)
    def _():
        o_ref[...] = jnp.zeros_like(o_ref)
        p_prev[...] = jnp.zeros_like(p_prev)
        for h in range(n_heads):
            e2_s[h] = jnp.exp(s2_ref[h] - st_ref[h, 2:3, :])

    live = (k < n_trips - 1).astype(F32)
    hn = hn_ref[...]

    hids = [_dot_nt(u_ref[part * half:(part + 1) * half, :], hn) for part in range(2)]
    first = _dot_tn(p_prev[...], vlo_ref[...])

    def gate_act(part):
        hid = hids[part]
        act = (0.5 * live) * hid * (1.0 + lax.erf(hid * (2.0 ** -0.5)))
        out = []
        for ii in range(half // PEER_KEYS):
            i_row = part * (half // PEER_KEYS) + ii
            wsum = None
            for h in range(n_heads):
                s1 = s1_ref[i_row, h:h + 1, :]
                sel = (s1 + s2_ref[h]) >= st_ref[h, 0:1, :]
                g = jnp.where(sel, jnp.exp(s1 - st_ref[h, 1:2, :]) * e2_s[h], 0.0)
                wsum = g if wsum is None else wsum + g
            out.append((wsum * act[ii * PEER_KEYS:(ii + 1) * PEER_KEYS, :]).astype(BF16))
        return jnp.concatenate(out, axis=0)

    pa = gate_act(0)
    second = _dot_tn(pa, vhi_ref[...])
    p_prev[...] = gate_act(1)
    o_ref[...] += first + second


def peer_dense(hn, eu, ev, s1, s2, st, tm, half):
    r, d = hn.shape
    n_exp = eu.shape[0]
    n_heads = s2.shape[0]
    n_pairs = n_exp // (2 * half)
    n_half = n_exp // half
    kern = functools.partial(_peer_dense_kernel, n_heads=n_heads, half=half, n_trips=n_pairs + 1)
    once = pl.Buffered(1)
    return pl.pallas_call(
        kern, grid=(pl.cdiv(r, tm), n_pairs + 1),
        in_specs=[pl.BlockSpec((tm, d), lambda i, k: (i, 0), pipeline_mode=once),
                  pl.BlockSpec((2 * half, d), lambda i, k: (jnp.minimum(k, n_pairs - 1), 0)),
                  pl.BlockSpec((half, d), lambda i, k: (jnp.maximum(2 * k - 1, 0), 0)),
                  pl.BlockSpec((half, d), lambda i, k: (jnp.minimum(2 * k, n_half - 1), 0)),
                  pl.BlockSpec((2 * half // PEER_KEYS, n_heads, tm),
                               lambda i, k: (jnp.minimum(k, n_pairs - 1), 0, i)),
                  pl.BlockSpec((n_heads, PEER_KEYS, tm), lambda i, k: (0, 0, i), pipeline_mode=once),
                  pl.BlockSpec((n_heads, 8, tm), lambda i, k: (0, 0, i), pipeline_mode=once)],
        out_specs=pl.BlockSpec((tm, d), lambda i, k: (i, 0), pipeline_mode=once),
        out_shape=jax.ShapeDtypeStruct((r, d), F32),
        scratch_shapes=[pltpu.VMEM((half, tm), BF16), pltpu.VMEM((n_heads, PEER_KEYS, tm), F32)],
        compiler_params=_cparams(("parallel", "arbitrary"), 56), name="peer_dense")(
            hn, eu, ev, ev, s1, s2, st)


def kernel(x_prompt, x_sample, cache_k, cache_v, state_ssm, state_conv, page_table, norm_mix, w_in,
           conv_w, conv_b, dt_bias, a_log, d_skip, ssm_norm_w, sb_bias, w_branch_a, w_branch_b,
           w_out, norm_ffn, w_query, sub_keys, expert_u, expert_v, norm_final):
    depth = w_in.shape[0]
    assert depth == 1 and x_prompt.shape[0] == 1
    _, t, d = x_prompt.shape
    n_seq, seq, _ = x_sample.shape
    n_tok = n_seq * seq
    r = t + n_tok
    n_ssm_heads = dt_bias.shape[1]
    d_inner = n_ssm_heads * HEAD_P
    g = n_ssm_heads // GROUP_HEADS
    conv_dim = conv_w.shape[2]
    n_sb = sb_bias.shape[1]
    sbw = n_sb * SB_DH
    assert t % 256 == 0 and n_tok == 128 and r % 640 == 0

    wl = w_in[0]
    dt0 = d_inner + conv_dim
    w_rest = realign_cast(wl, dt0 + n_ssm_heads)
    w_dt = jnp.pad(wl[:, dt0:dt0 + n_ssm_heads], ((0, 0), (0, 128 - n_ssm_heads))).astype(BF16)

    x_all = jnp.concatenate([x_prompt[0], x_sample.reshape(n_tok, d)], axis=0)
    xn = rmsnorm(x_all, norm_mix[0], BF16)
    tm5 = r // 5
    (proj,) = matmul(xn, wl, [F32], tm5, 256, col0=0, n=dt0)
    qscale = SB_DH ** -0.5 * LOG2E
    tm_p = max(c for c in range(128, 1025, 128) if t % c == 0)
    prm = dict(tm=tm_p, tn=512, row0=0, m=t)
    smp = dict(tm=n_tok, tn=512, row0=t, m=n_tok)
    (q_p,) = matmul(xn, w_rest, [BF16], col0=0, n=sbw, scale=qscale, **prm)
    (q_s,) = matmul(xn, w_rest, [BF16], col0=0, n=sbw, scale=qscale, **smp)
    kp_f, kp_bf = matmul(xn, w_rest, [F32, BF16], col0=sbw, n=sbw, **prm)
    (k_s,) = matmul(xn, w_rest, [F32], col0=sbw, n=sbw, **smp)
    vp_f, vp_bf = matmul(xn, w_rest, [F32, BF16], col0=2 * sbw, n=sbw, **prm)
    (v_s,) = matmul(xn, w_rest, [F32], col0=2 * sbw, n=sbw, **smp)
    (gates,) = matmul(xn, w_rest, [F32], tm5, 512, col0=3 * sbw, n=2 * d)
    dt_raw = matmul(xn, w_dt, [F32], tm5, 128)[0][:, :n_ssm_heads]

    def head_layouts(v):
        rows = v.shape[0]
        vg = v.reshape(rows, g, GROUP_HEADS).transpose(1, 0, 2)
        nat = jnp.pad(vg, ((0, 0), (0, 0), (0, 128 - GROUP_HEADS)))
        tr = jnp.pad(vg.transpose(0, 2, 1), ((0, 0), (0, 16 - GROUP_HEADS), (0, 0)))
        return nat, tr

    hp = jnp.stack([dt_bias[0], a_log[0], d_skip[0]], axis=0)
    hp_g = hp.reshape(3, g, GROUP_HEADS).transpose(1, 0, 2)
    hpn = jnp.pad(hp_g, ((0, 0), (0, 5), (0, 128 - GROUP_HEADS)))
    hpt = jnp.pad(hp_g.transpose(0, 2, 1), ((0, 0), (0, 16 - GROUP_HEADS), (0, 125)))
    nw = ssm_norm_w[0].reshape(1, d_inner)
    cw, cbias = conv_w[0], conv_b[0].reshape(1, conv_dim)

    dtn_p, dtt_p = head_layouts(dt_raw[:t])
    ya_head, ssm_p = ssd_prompt(proj, t, dtn_p, dtt_p, cw, cbias, hpn, hpt, nw, L=256)

    xbc_s = proj[t:, d_inner:d_inner + conv_dim].reshape(n_seq, seq, conv_dim)
    ext_s = jnp.concatenate([state_conv[0], xbc_s], axis=1)
    sh = jnp.stack([ext_s[:, k:k + seq].reshape(n_tok, conv_dim) for k in range(4)], axis=0)
    dtn_s, dtt_s = head_layouts(dt_raw[t:])
    ya, ssm_s = ssd_sample(sh, proj, t // n_tok, dtn_s, dtt_s, cw, cbias, hpn, hpt, nw,
                           state_ssm[0].reshape(n_seq, d_inner, D_STATE), seq, ya_head)

    yb_p = sb_prompt(q_p, kp_bf, vp_bf, t, sb_bias[0], blk=256, hb=8)
    n_phys, page = cache_k.shape[1], cache_k.shape[2]
    yb_s = sb_sample(q_s, k_s, v_s, cache_k.reshape(n_phys, page * n_sb, SB_DH),
                     cache_v.reshape(n_phys, page * n_sb, SB_DH), page_table, sb_bias[0], n_sb, seq)

    yb = jnp.concatenate([yb_p, yb_s.reshape(n_tok, sbw)], axis=0)
    merged = merge_branches(ya, yb, w_branch_a[0].astype(BF16), w_branch_b[0].astype(BF16),
                            gates, 0, d, tm=640, tn=512)
    h1 = matmul_resid(merged, w_out[0].astype(BF16), x_all, tm=640, tn=512)

    hn = rmsnorm(h1, norm_ffn[0], BF16)
    (qp,) = matmul(hn, w_query[0].astype(BF16), [BF16], tm5, 512)
    s1_hk, s2, stats = peer_topk(qp, sub_keys[0], tm=128)
    s1 = s1_hk.transpose(1, 0, 2)
    peer = peer_dense(hn, expert_u[0].astype(BF16), expert_v[0].astype(BF16), s1, s2, stats,
                      tm=768 if r > 768 else r, half=256)
    y_p, y_s = rmsnorm_add_split(h1, peer, norm_final, t)

    y_prompt = y_p.reshape(1, t, d)
    y_sample = y_s.reshape(n_seq, seq, d)
    k_prompt = kp_f.reshape(1, 1, t, n_sb, SB_DH)
    v_prompt = vp_f.reshape(1, 1, t, n_sb, SB_DH)
    ssm_prompt = ssm_p.reshape(1, 1, n_ssm_heads, HEAD_P, D_STATE)
    conv_prompt = proj[t - 3:t, d_inner:d_inner + conv_dim].reshape(1, 1, 3, conv_dim)
    k_sample = k_s.reshape(1, n_seq, seq, n_sb, SB_DH)
    v_sample = v_s.reshape(1, n_seq, seq, n_sb, SB_DH)
    ssm_sample = ssm_s.reshape(1, n_seq, n_ssm_heads, HEAD_P, D_STATE)
    conv_sample = ext_s[:, seq:].reshape(1, n_seq, 3, conv_dim)
    return (y_prompt, y_sample, k_prompt, v_prompt, ssm_prompt, conv_prompt,
            k_sample, v_sample, ssm_sample, conv_sample)
```

```python
import functools
import math

import numpy as np
import jax
import jax.numpy as jnp
from jax import lax
from jax.experimental import pallas as pl
from jax.experimental.pallas import tpu as pltpu

F32 = jnp.float32
BF16 = jnp.bfloat16
EPS = 1e-6
NEG_BIG = -1e30

HEAD_P = 64
GROUP_HEADS = 8
GROUP_W = HEAD_P * GROUP_HEADS
D_STATE = 128
SB_DH = 128
PEER_K = 16
PEER_KEYS = 128


def _cparams(sem, vmem_mb):
    return pltpu.CompilerParams(dimension_semantics=sem,
                                vmem_limit_bytes=vmem_mb * 1024 * 1024)


def _softplus(x):
    return jnp.maximum(x, 0.0) + jnp.log1p(jnp.exp(-jnp.abs(x)))


LOG2E = 1.4426950408889634


def _softplus2(x):
    return jnp.where(x > 64.0, x, jnp.log(1.0 + jnp.exp2(x)) * LOG2E)


def _silu(x):
    return x * jax.nn.sigmoid(x)


def _split2(x):
    hi = x.astype(BF16)
    lo = (x - hi.astype(F32)).astype(BF16)
    return hi, lo


def _split3(x):
    hi = x.astype(BF16)
    r = x - hi.astype(F32)
    mid = r.astype(BF16)
    lo = (r - mid.astype(F32)).astype(BF16)
    return hi, mid, lo


def _dot(a, b):
    return jnp.dot(a, b, preferred_element_type=F32)


def _dot_nt(a, b):
    return lax.dot_general(a, b, (((1,), (1,)), ((), ())), preferred_element_type=F32)


def _dot_tn(a, b):
    return lax.dot_general(a, b, (((0,), (0,)), ((), ())), preferred_element_type=F32)


def _rms_kernel(x_ref, w_ref, o_ref):
    x = x_ref[...]
    r = x * lax.rsqrt(jnp.mean(x * x, axis=-1, keepdims=True) + EPS)
    o_ref[...] = (r * w_ref[...]).astype(o_ref.dtype)


def rmsnorm(x, w, out_dtype, tm=128):
    m, d = x.shape
    row = pl.BlockSpec((tm, d), lambda i: (i, 0))
    return pl.pallas_call(
        _rms_kernel, grid=(m // tm,),
        in_specs=[row, pl.BlockSpec((1, d), lambda i: (0, 0))], out_specs=row,
        out_shape=jax.ShapeDtypeStruct((m, d), out_dtype),
        compiler_params=_cparams(("parallel",), 40), name="rmsnorm")(x, w.reshape(1, d))


def _rms_add_split_kernel(x_ref, y_ref, w_ref, head_ref, tail_ref, *, head_blocks):
    i = pl.program_id(0)
    x = x_ref[...] + y_ref[...]
    r = x * lax.rsqrt(jnp.mean(x * x, axis=-1, keepdims=True) + EPS)
    out = r * w_ref[...]

    @pl.when(i < head_blocks)
    def _():
        head_ref[...] = out

    @pl.when(i >= head_blocks)
    def _():
        tail_ref[...] = out


def rmsnorm_add_split(x, y, w, n_head, tm=128):
    m, d = x.shape
    hb = n_head // tm
    row = pl.BlockSpec((tm, d), lambda i: (i, 0))
    return pl.pallas_call(
        functools.partial(_rms_add_split_kernel, head_blocks=hb), grid=(m // tm,),
        in_specs=[row, row, pl.BlockSpec((1, d), lambda i: (0, 0))],
        out_specs=[pl.BlockSpec((tm, d), lambda i: (jnp.minimum(i, hb - 1), 0)),
                   pl.BlockSpec((tm, d), lambda i: (jnp.maximum(i - hb, 0), 0))],
        out_shape=[jax.ShapeDtypeStruct((n_head, d), F32),
                   jax.ShapeDtypeStruct((m - n_head, d), F32)],
        compiler_params=_cparams(("arbitrary",), 40), name="rmsnorm_out")(x, y, w.reshape(1, d))


def _mm_kernel(a_ref, w_ref, *o_refs, scale, w_rows):
    w = w_ref[...].astype(BF16)
    acc = _dot_nt(a_ref[...], w) if w_rows else _dot(a_ref[...], w)
    if scale is not None:
        acc = acc * scale
    for o_ref in o_refs:
        o_ref[...] = acc.astype(o_ref.dtype)


def matmul(a, w, out_dtypes, tm, tn, col0=0, n=None, scale=None, row0=0, m=None, w_rows=False):
    k = a.shape[1]
    m = a.shape[0] if m is None else m
    if n is None:
        n = w.shape[0] if w_rows else w.shape[1]
    c0, r0 = col0 // tn, row0 // tm
    out = pl.BlockSpec((tm, tn), lambda i, j: (i, j))
    wspec = (pl.BlockSpec((tn, k), lambda i, j: (c0 + j, 0)) if w_rows
             else pl.BlockSpec((k, tn), lambda i, j: (0, c0 + j)))
    return pl.pallas_call(
        functools.partial(_mm_kernel, scale=scale, w_rows=w_rows), grid=(m // tm, n // tn),
        in_specs=[pl.BlockSpec((tm, k), lambda i, j: (r0 + i, 0)), wspec],
        out_specs=[out] * len(out_dtypes),
        out_shape=[jax.ShapeDtypeStruct((m, n), dt) for dt in out_dtypes],
        compiler_params=_cparams(("parallel", "arbitrary"), 52), name="matmul")(a, w)


def _mm_resid_kernel(a_ref, w_ref, x_ref, o_ref):
    o_ref[...] = x_ref[...] + _dot(a_ref[...], w_ref[...])


def matmul_resid(a, w, x, tm, tn):
    m, k = a.shape
    n = w.shape[1]
    return pl.pallas_call(
        _mm_resid_kernel, grid=(m // tm, n // tn),
        in_specs=[pl.BlockSpec((tm, k), lambda i, j: (i, 0)),
                  pl.BlockSpec((k, tn), lambda i, j: (0, j)),
                  pl.BlockSpec((tm, tn), lambda i, j: (i, j))],
        out_specs=pl.BlockSpec((tm, tn), lambda i, j: (i, j)),
        out_shape=jax.ShapeDtypeStruct((m, n), F32),
        compiler_params=_cparams(("parallel", "arbitrary"), 52), name="matmul_resid")(a, w, x)


def _mm_merge_kernel(ya_ref, yb_ref, wa_ref, wb_ref, ga_ref, gb_ref, o_ref):
    a = _dot(ya_ref[...], wa_ref[...])
    b = _dot(yb_ref[...], wb_ref[...])
    o_ref[...] = (jax.nn.sigmoid(ga_ref[...]) * a + jax.nn.sigmoid(gb_ref[...]) * b).astype(o_ref.dtype)


def merge_branches(ya, yb, wa, wb, proj, ga_col, gb_col, tm, tn):
    m, ka = ya.shape
    kb = yb.shape[1]
    n = wa.shape[1]
    ga0, gb0 = ga_col // tn, gb_col // tn
    return pl.pallas_call(
        _mm_merge_kernel, grid=(m // tm, n // tn),
        in_specs=[pl.BlockSpec((tm, ka), lambda i, j: (i, 0)),
                  pl.BlockSpec((tm, kb), lambda i, j: (i, 0)),
                  pl.BlockSpec((ka, tn), lambda i, j: (0, j)),
                  pl.BlockSpec((kb, tn), lambda i, j: (0, j)),
                  pl.BlockSpec((tm, tn), lambda i, j: (i, ga0 + j)),
                  pl.BlockSpec((tm, tn), lambda i, j: (i, gb0 + j))],
        out_specs=pl.BlockSpec((tm, tn), lambda i, j: (i, j)),
        out_shape=jax.ShapeDtypeStruct((m, n), BF16),
        compiler_params=_cparams(("parallel", "arbitrary"), 52), name="merge")(
            ya, yb, wa, wb, proj, proj)


def _conv_silu(ext, w_ref, b_ref, L):
    acc = b_ref[...] + w_ref[3:4, :] * ext[8:8 + L, :]
    acc = acc + w_ref[2:3, :] * ext[7:7 + L, :]
    acc = acc + w_ref[1:2, :] * ext[6:6 + L, :]
    acc = acc + w_ref[0:1, :] * ext[5:5 + L, :]
    return _silu(acc)


def _pair_terms(pr, x, dt, acs, acs_t, cb, ch, mask, hpn, lo_half):
    ra, rb = 2 * pr, 2 * pr + 1
    cola, colb = acs[:, ra:ra + 1], acs[:, rb:rb + 1]
    rowa, rowb = acs_t[ra:ra + 1, :], acs_t[rb:rb + 1, :]
    ma = cb * jnp.exp(jnp.where(mask, cola - rowa, NEG_BIG))
    mb = cb * jnp.exp(jnp.where(mask, colb - rowb, NEG_BIG))
    mcat = jnp.concatenate([ma.astype(BF16), mb.astype(BF16)], axis=1)
    xp = x[:, pr * 128:(pr + 1) * 128]
    xdt = xp * jnp.where(lo_half, dt[:, ra:ra + 1], dt[:, rb:rb + 1])
    rhs = jnp.concatenate([jnp.where(lo_half, xdt, 0.0), jnp.where(lo_half, 0.0, xdt)],
                          axis=0).astype(BF16)
    yd = _dot(mcat, rhs)
    eap = jnp.where(lo_half, jnp.exp(cola), jnp.exp(colb))
    dsk = jnp.where(lo_half[0:1, :], hpn[2:3, ra:ra + 1], hpn[2:3, rb:rb + 1])
    y = yd + ch[:, pr * 128:(pr + 1) * 128] * eap + xp * dsk
    return y, xdt, eap


def _ssd_prompt_kernel(xs_ref, b_ref, c_ref, z_ref, dtn_ref, dtt_ref,
                       cwx_ref, cwb_ref, cwc_ref, cbx_ref, cbb_ref, cbc_ref,
                       hpn_ref, hpt_ref, nw_ref, lin_ref, uin_ref,
                       y_ref, hout_ref, extx, extb, extc, hs, *, L):
    c = pl.program_id(1)

    @pl.when(c == 0)
    def _():
        extx[0:8, :] = jnp.zeros((8, GROUP_W), F32)
        extb[0:8, :] = jnp.zeros((8, D_STATE), F32)
        extc[0:8, :] = jnp.zeros((8, D_STATE), F32)
        hs[...] = jnp.zeros_like(hs)

    extx[8:8 + L, :] = xs_ref[...]
    extb[8:8 + L, :] = b_ref[...]
    extc[8:8 + L, :] = c_ref[...]
    x = _conv_silu(extx, cwx_ref, cbx_ref, L)
    bm = _conv_silu(extb, cwb_ref, cbb_ref, L).astype(BF16)
    cm = _conv_silu(extc, cwc_ref, cbc_ref, L).astype(BF16)
    extx[0:8, :] = extx[L:L + 8, :]
    extb[0:8, :] = extb[L:L + 8, :]
    extc[0:8, :] = extc[L:L + 8, :]

    hpn = hpn_ref[0]
    hpt = hpt_ref[0]
    dt = _softplus(dtn_ref[0] + hpn[0:1, :])
    d_a = dt * (-jnp.exp(hpn[1:2, :]))
    d_at = _softplus(dtt_ref[0] + hpt[:, 0:1]) * (-jnp.exp(hpt[:, 1:2]))
    lin = lin_ref[...]
    uin = uin_ref[...]
    acs = sum(_dot(lin, p) for p in _split3(d_a))
    acs_t = sum(_dot(p, uin) for p in _split3(d_at))

    rows = lax.broadcasted_iota(jnp.int32, (L, L), 0)
    cols = lax.broadcasted_iota(jnp.int32, (L, L), 1)
    causal = rows >= cols
    lo_half = lax.broadcasted_iota(jnp.int32, (L, 128), 1) < HEAD_P
    sub_lo = lax.broadcasted_iota(jnp.int32, (128, D_STATE), 0) < HEAD_P

    cb = _dot_nt(cm, bm)
    ch = _dot_nt(cm, hs[...].astype(BF16))

    ys = []
    for pr in range(GROUP_HEADS // 2):
        ra, rb = 2 * pr, 2 * pr + 1
        y, xdt, _ = _pair_terms(pr, x, dt, acs, acs_t, cb, ch, causal, hpn, lo_half)
        ys.append(y)
        lasta = acs_t[ra:ra + 1, L - 1:L]
        lastb = acs_t[rb:rb + 1, L - 1:L]
        dend = jnp.where(lo_half, jnp.exp(lasta - acs[:, ra:ra + 1]),
                         jnp.exp(lastb - acs[:, rb:rb + 1]))
        s_new = _dot_tn((xdt * dend).astype(BF16), bm)
        cd = jnp.where(sub_lo, jnp.exp(lasta), jnp.exp(lastb))
        sl = slice(pr * 128, (pr + 1) * 128)
        hs[sl, :] = hs[sl, :] * cd + s_new

    yg = jnp.concatenate(ys, axis=1)
    u = yg * _silu(z_ref[...])
    u = u * lax.rsqrt(jnp.mean(u * u, axis=-1, keepdims=True) + EPS)
    y_ref[...] = (u * nw_ref[...]).astype(y_ref.dtype)

    @pl.when(c == pl.num_programs(1) - 1)
    def _():
        hout_ref[...] = hs[...]


def _tri_incl(n):
    return np.tril(np.ones((n, n), np.float32))


def ssd_prompt(proj, t, dt_n, dt_t, conv_w, conv_b, hpn, hpt, norm_w, L):
    g = dt_n.shape[0]
    nc = t // L
    d_inner = g * GROUP_W
    xb = d_inner // GROUP_W
    bb = (2 * d_inner) // D_STATE
    cbk = bb + g
    lin = jnp.asarray(_tri_incl(L), BF16)
    uin = jnp.asarray(_tri_incl(L).T, BF16)
    const = lambda gi, ci: (0, 0)
    kern = functools.partial(_ssd_prompt_kernel, L=L)
    return pl.pallas_call(
        kern, grid=(g, nc),
        in_specs=[
            pl.BlockSpec((L, GROUP_W), lambda gi, ci: (ci, xb + gi)),
            pl.BlockSpec((L, D_STATE), lambda gi, ci: (ci, bb + gi)),
            pl.BlockSpec((L, D_STATE), lambda gi, ci: (ci, cbk + gi)),
            pl.BlockSpec((L, GROUP_W), lambda gi, ci: (ci, gi)),
            pl.BlockSpec((1, L, 128), lambda gi, ci: (gi, ci, 0)),
            pl.BlockSpec((1, 16, L), lambda gi, ci: (gi, 0, ci)),
            pl.BlockSpec((4, GROUP_W), lambda gi, ci: (0, gi)),
            pl.BlockSpec((4, D_STATE), lambda gi, ci: (0, bb - xb * 4 + gi)),
            pl.BlockSpec((4, D_STATE), lambda gi, ci: (0, bb - xb * 4 + g + gi)),
            pl.BlockSpec((1, GROUP_W), lambda gi, ci: (0, gi)),
            pl.BlockSpec((1, D_STATE), lambda gi, ci: (0, bb - xb * 4 + gi)),
            pl.BlockSpec((1, D_STATE), lambda gi, ci: (0, bb - xb * 4 + g + gi)),
            pl.BlockSpec((1, 8, 128), lambda gi, ci: (gi, 0, 0)),
            pl.BlockSpec((1, 16, 128), lambda gi, ci: (gi, 0, 0)),
            pl.BlockSpec((1, GROUP_W), lambda gi, ci: (0, gi)),
            pl.BlockSpec((L, L), const),
            pl.BlockSpec((L, L), const),
        ],
        out_specs=[pl.BlockSpec((L, GROUP_W), lambda gi, ci: (ci, gi)),
                   pl.BlockSpec((GROUP_W, D_STATE), lambda gi, ci: (gi, 0))],
        out_shape=[jax.ShapeDtypeStruct((proj.shape[0], d_inner), BF16),
                   jax.ShapeDtypeStruct((d_inner, D_STATE), F32)],
        scratch_shapes=[pltpu.VMEM((L + 8, GROUP_W), F32), pltpu.VMEM((L + 8, D_STATE), F32),
                        pltpu.VMEM((L + 8, D_STATE), F32), pltpu.VMEM((GROUP_W, D_STATE), F32)],
        compiler_params=_cparams(("arbitrary", "arbitrary"), 40), name="ssd_prompt")(
            proj, proj, proj, proj, dt_n, dt_t, conv_w, conv_w, conv_w, conv_b, conv_b, conv_b,
            hpn, hpt, norm_w, lin, uin)


def _ssd_sample_kernel(shx_ref, shb_ref, shc_ref, z_ref, dtn_ref, dtt_ref,
                       cwx_ref, cwb_ref, cwc_ref, cbx_ref, cbb_ref, cbc_ref,
                       hpn_ref, hpt_ref, nw_ref, lseg_ref, useg_ref, sseg_ref, st_ref, y_all_ref,
                       y_ref, hout_ref, yacc, cm_s, bm_s, ea_s, xwt_s, cdl_s, *, n_tok, seq):
    b = pl.program_id(1)

    @pl.when(b == 0)
    def _():
        def conv(sh_ref, w_ref, b_ref):
            acc = b_ref[...] + w_ref[0:1, :] * sh_ref[0]
            for k in range(1, 4):
                acc = acc + w_ref[k:k + 1, :] * sh_ref[k]
            return _silu(acc)

        x = conv(shx_ref, cwx_ref, cbx_ref)
        bm = conv(shb_ref, cwb_ref, cbb_ref).astype(BF16)
        cm = conv(shc_ref, cwc_ref, cbc_ref).astype(BF16)
        hpn = hpn_ref[0]
        hpt = hpt_ref[0]
        dt = _softplus(dtn_ref[0] + hpn[0:1, :])
        d_a = dt * (-jnp.exp(hpn[1:2, :]))
        d_at = _softplus(dtt_ref[0] + hpt[:, 0:1]) * (-jnp.exp(hpt[:, 1:2]))
        lseg = lseg_ref[...]
        useg = useg_ref[...]
        sseg = sseg_ref[...]
        pa, pat = _split3(d_a), _split3(d_at)
        acs = sum(_dot(lseg, p) for p in pa)
        tot = sum(_dot(sseg, p) for p in pa)
        acs_t = sum(_dot(p, useg) for p in pat)
        tot_t = sum(_dot(p, sseg) for p in pat)
        mask = lseg > 0
        lo_half = lax.broadcasted_iota(jnp.int32, (n_tok, 128), 1) < HEAD_P
        cb = _dot_nt(cm, bm)
        zero_ch = jnp.zeros((n_tok, GROUP_W), F32)
        ys, xws, eas = [], [], []
        for pr in range(GROUP_HEADS // 2):
            ra, rb = 2 * pr, 2 * pr + 1
            y, xdt, eap = _pair_terms(pr, x, dt, acs, acs_t, cb, zero_ch, mask, hpn, lo_half)
            dend = jnp.where(lo_half, jnp.exp(tot[:, ra:ra + 1] - acs[:, ra:ra + 1]),
                             jnp.exp(tot[:, rb:rb + 1] - acs[:, rb:rb + 1]))
            ys.append(y)
            xws.append(xdt * dend)
            eas.append(eap)
        yacc[...] = jnp.concatenate(ys, axis=1)
        ea_s[...] = jnp.concatenate(eas, axis=1)
        xwt_s[...] = jnp.transpose(jnp.concatenate(xws, axis=1)).astype(BF16)
        cm_s[...] = cm
        bm_s[...] = bm
        cdl_s[...] = jnp.concatenate(
            [jnp.broadcast_to(tot_t[r:r + 1, :], (HEAD_P, n_tok)) for r in range(GROUP_HEADS)], axis=0)

    h0 = st_ref[0]
    lo = b * seq
    rid = lax.broadcasted_iota(jnp.int32, (n_tok, GROUP_W), 0)
    in_rows = jnp.abs(2 * (rid - lo) - (seq - 1)) < seq
    lid = lax.broadcasted_iota(jnp.int32, (GROUP_W, n_tok), 1)
    in_lanes = jnp.abs(2 * (lid - lo) - (seq - 1)) < seq
    ch = _dot_nt(cm_s[...], h0.astype(BF16))
    yacc[...] += jnp.where(in_rows, ch * ea_s[...], 0.0)
    s_new = _dot(jnp.where(in_lanes, xwt_s[...], jnp.zeros_like(xwt_s[...])), bm_s[...])
    cdcol = jnp.sum(jnp.where(lid == lo, cdl_s[...], 0.0), axis=-1, keepdims=True)
    hout_ref[0] = h0 * jnp.exp(cdcol) + s_new

    @pl.when(b == pl.num_programs(1) - 1)
    def _():
        u = yacc[...] * _silu(z_ref[...])
        u = u * lax.rsqrt(jnp.mean(u * u, axis=-1, keepdims=True) + EPS)
        y_ref[...] = (u * nw_ref[...]).astype(y_ref.dtype)


def ssd_sample(sh, proj, z_row_block, dt_n, dt_t, conv_w, conv_b, hpn, hpt, norm_w, state, seq,
               y_all):
    g = dt_n.shape[0]
    n_tok = sh.shape[1]
    n_seq = n_tok // seq
    d_inner = g * GROUP_W
    xb4 = d_inner // D_STATE
    sid = np.arange(n_tok) // seq
    same = (sid[:, None] == sid[None, :]).astype(np.float32)
    lseg = jnp.asarray(same * _tri_incl(n_tok), BF16)
    useg = jnp.asarray((same * _tri_incl(n_tok)).T, BF16)
    sseg = jnp.asarray(same, BF16)
    const = lambda gi, bi: (0, 0)
    kern = functools.partial(_ssd_sample_kernel, n_tok=n_tok, seq=seq)
    return pl.pallas_call(
        kern, grid=(g, n_seq),
        in_specs=[
            pl.BlockSpec((4, n_tok, GROUP_W), lambda gi, bi: (0, 0, gi)),
            pl.BlockSpec((4, n_tok, D_STATE), lambda gi, bi: (0, 0, xb4 + gi)),
            pl.BlockSpec((4, n_tok, D_STATE), lambda gi, bi: (0, 0, xb4 + g + gi)),
            pl.BlockSpec((n_tok, GROUP_W), lambda gi, bi: (z_row_block, gi)),
            pl.BlockSpec((1, n_tok, 128), lambda gi, bi: (gi, 0, 0)),
            pl.BlockSpec((1, 16, n_tok), lambda gi, bi: (gi, 0, 0)),
            pl.BlockSpec((4, GROUP_W), lambda gi, bi: (0, gi)),
            pl.BlockSpec((4, D_STATE), lambda gi, bi: (0, xb4 + gi)),
            pl.BlockSpec((4, D_STATE), lambda gi, bi: (0, xb4 + g + gi)),
            pl.BlockSpec((1, GROUP_W), lambda gi, bi: (0, gi)),
            pl.BlockSpec((1, D_STATE), lambda gi, bi: (0, xb4 + gi)),
            pl.BlockSpec((1, D_STATE), lambda gi, bi: (0, xb4 + g + gi)),
            pl.BlockSpec((1, 8, 128), lambda gi, bi: (gi, 0, 0)),
            pl.BlockSpec((1, 16, 128), lambda gi, bi: (gi, 0, 0)),
            pl.BlockSpec((1, GROUP_W), lambda gi, bi: (0, gi)),
            pl.BlockSpec((n_tok, n_tok), const),
            pl.BlockSpec((n_tok, n_tok), const),
            pl.BlockSpec((n_tok, n_tok), const),
            pl.BlockSpec((1, GROUP_W, D_STATE), lambda gi, bi: (bi, gi, 0)),
            pl.BlockSpec(memory_space=pl.ANY),
        ],
        out_specs=[pl.BlockSpec((n_tok, GROUP_W), lambda gi, bi: (z_row_block, gi)),
                   pl.BlockSpec((1, GROUP_W, D_STATE), lambda gi, bi: (bi, gi, 0))],
        out_shape=[jax.ShapeDtypeStruct(y_all.shape, BF16),
                   jax.ShapeDtypeStruct(state.shape, F32)],
        input_output_aliases={19: 0},
        scratch_shapes=[pltpu.VMEM((n_tok, GROUP_W), F32), pltpu.VMEM((n_tok, D_STATE), BF16),
                        pltpu.VMEM((n_tok, D_STATE), BF16), pltpu.VMEM((n_tok, GROUP_W), F32),
                        pltpu.VMEM((GROUP_W, n_tok), BF16), pltpu.VMEM((GROUP_W, n_tok), F32)],
        compiler_params=_cparams(("arbitrary", "arbitrary"), 40), name="ssd_sample")(
            sh, sh, sh, proj, dt_n, dt_t, conv_w, conv_w, conv_w, conv_b, conv_b, conv_b,
            hpn, hpt, norm_w, lseg, useg, sseg, state, y_all)


def _sb_prompt_kernel(bias_ref, q_ref, k_ref, v_ref, u_ref, o_ref, *, blk, hb):
    g = pl.program_id(0)
    i = pl.program_id(1)
    umat = u_ref[...]
    rows = lax.broadcasted_iota(jnp.int32, (blk, blk), 0)
    cols = lax.broadcasted_iota(jnp.int32, (blk, blk), 1)
    strict = cols < rows

    def block(j, carry, masked):
        start = pl.multiple_of(j * blk, blk)
        heads = range(hb)
        hsl = [slice(h * SB_DH, (h + 1) * SB_DH) for h in heads]
        zs = [_dot_nt(q_ref[:, hsl[h]], k_ref[pl.ds(start, blk), hsl[h]])
              + bias_ref[g * hb + h] * LOG2E for h in heads]
        sps = [_softplus2(z) for z in zs]
        if masked:
            sps = [jnp.where(strict, sp, 0.0) for sp in sps]
        sufs = [_dot(sp.astype(BF16), umat) for sp in sps]
        ws = [jnp.exp2(zs[h] - sps[h] - sufs[h] - carry[h][1]) for h in heads]
        if masked:
            ws = [jnp.where(strict, w, 0.0) for w in ws]
        accs = [carry[h][0] + _dot(ws[h].astype(BF16), v_ref[pl.ds(start, blk), hsl[h]])
                for h in heads]
        runs = [carry[h][1] + jnp.sum(sps[h], axis=-1, keepdims=True) for h in heads]
        return tuple(zip(accs, runs))

    carry = tuple((jnp.zeros((blk, SB_DH), F32), jnp.zeros((blk, 1), F32)) for _ in range(hb))
    carry = block(i, carry, True)
    carry = lax.fori_loop(0, i, lambda t, c: block(i - 1 - t, c, False), carry)
    for h in range(hb):
        o_ref[:, h * SB_DH:(h + 1) * SB_DH] = carry[h][0].astype(o_ref.dtype)


def _tri_later(n):
    return np.tril(np.ones((n, n), np.float32), -1)


def sb_prompt(q, k, v, t, bias, blk, hb):
    width = q.shape[1]
    gw = hb * SB_DH
    umat = jnp.asarray(_tri_later(blk), BF16)
    kern = functools.partial(_sb_prompt_kernel, blk=blk, hb=hb)
    return pl.pallas_call(
        kern, grid=(width // gw, t // blk),
        in_specs=[
            pl.BlockSpec(memory_space=pltpu.SMEM),
            pl.BlockSpec((blk, gw), lambda g, i: (i, g)),
            pl.BlockSpec((t, gw), lambda g, i: (0, g), pipeline_mode=pl.Buffered(1)),
            pl.BlockSpec((t, gw), lambda g, i: (0, g), pipeline_mode=pl.Buffered(1)),
            pl.BlockSpec((blk, blk), lambda g, i: (0, 0)),
        ],
        out_specs=pl.BlockSpec((blk, gw), lambda g, i: (i, g)),
        out_shape=jax.ShapeDtypeStruct((t, width), BF16),
        compiler_params=_cparams(("arbitrary", "arbitrary"), 48), name="sb_prompt")(
            bias, q, k, v, umat)


def _sb_sample_kernel(pt_ref, qbd_ref, brow_ref, kn_ref, vn_ref, tn_ref, tp_ref, *rest,
                      n_heads, seq, page, ppb, grp):
    k_pages, v_pages = rest[:ppb], rest[ppb:2 * ppb]
    o_ref, acc, run = rest[2 * ppb:2 * ppb + 3]
    bufs = rest[2 * ppb + 3:]
    s = pl.program_id(1)
    qbd = qbd_ref[0]
    brow = brow_ref[...]
    nq = n_heads * 8

    def process(kb, vb, tmat, mask):
        z = _dot(kb, qbd) + brow
        sp = _softplus2(z)
        if mask is not None:
            sp = jnp.where(mask, sp, 0.0)
        hi, lo = _split2(sp)
        suffix = _dot(tmat, hi) + _dot(tmat, lo)
        w = jnp.exp2(z - sp - suffix - run[...])
        if mask is not None:
            w = jnp.where(mask, w, 0.0)
        acc[...] += _dot_tn(w.astype(BF16), vb)
        run[...] += jnp.sum(sp, axis=0, keepdims=True)

    @pl.when(s == 0)
    def _():
        acc[...] = jnp.zeros_like(acc)
        run[...] = jnp.zeros_like(run)
        fill = jnp.zeros((page - 8, n_heads * SB_DH), F32)
        knew = jnp.concatenate([kn_ref[0], fill], axis=0).astype(BF16)
        vnew = jnp.concatenate([vn_ref[0], fill], axis=0).astype(BF16)
        key = lax.broadcasted_iota(jnp.int32, (page, nq), 0)
        qpos = lax.broadcasted_iota(jnp.int32, (page, nq), 1) % 8
        process(knew, vnew, tn_ref[...], key < qpos)

    for gi in range(ppb // grp):
        kc, vc = bufs[2 * gi], bufs[2 * gi + 1]
        for u in range(grp):
            kp, vp = k_pages[gi * grp + u], v_pages[gi * grp + u]
            r0 = (grp - 1 - u) * page
            for h in range(n_heads):
                hs = slice(h * SB_DH, (h + 1) * SB_DH)
                kc[r0:r0 + page, hs] = kp[0, pl.ds(h, page, stride=n_heads), :].astype(BF16)
                vc[r0:r0 + page, hs] = vp[0, pl.ds(h, page, stride=n_heads), :].astype(BF16)
        process(kc[...], vc[...], tp_ref[...], None)

    @pl.when(s == pl.num_programs(1) - 1)
    def _():
        for h in range(n_heads):
            o_ref[0, :, h * SB_DH:(h + 1) * SB_DH] = (
                acc[h * 8:h * 8 + seq, h * SB_DH:(h + 1) * SB_DH].astype(o_ref.dtype))


def sb_sample(q_s, k_s, v_s, cache_k, cache_v, page_table, bias, n_heads, seq):
    n_seq, n_pages = page_table.shape
    page = cache_k.shape[1] // n_heads
    width = n_heads * SB_DH
    nq = n_heads * 8
    grp = 4
    ppb = 8 if n_pages % 8 == 0 else 4
    steps = n_pages // ppb
    q4 = jnp.pad(q_s.reshape(n_seq, seq, n_heads, SB_DH), ((0, 0), (0, 8 - seq), (0, 0), (0, 0)))
    eye = jnp.eye(n_heads, dtype=q_s.dtype)
    qbd = jnp.einsum("bthd,hg->bhdgt", q4, eye).reshape(n_seq, width, nq).astype(BF16)
    brow = jnp.repeat(bias * LOG2E, 8).reshape(1, nq)
    pad = lambda a: jnp.pad(a.reshape(n_seq, seq, width), ((0, 0), (0, 8 - seq), (0, 0)))
    tn = jnp.asarray(_tri_later(page).T, BF16)
    tp = jnp.asarray(_tri_later(grp * page).T, BF16)

    def page_spec(u):
        return pl.BlockSpec((1, page * n_heads, SB_DH),
                            lambda b, s, pt: (pt[b, n_pages - 1 - (ppb * s + u)], 0, 0))

    kern = functools.partial(_sb_sample_kernel, n_heads=n_heads, seq=seq, page=page, ppb=ppb,
                             grp=grp)
    grid_spec = pltpu.PrefetchScalarGridSpec(
        num_scalar_prefetch=1, grid=(n_seq, steps),
        in_specs=[
            pl.BlockSpec((1, width, nq), lambda b, s, pt: (b, 0, 0)),
            pl.BlockSpec((1, nq), lambda b, s, pt: (0, 0)),
            pl.BlockSpec((1, 8, width), lambda b, s, pt: (b, 0, 0)),
            pl.BlockSpec((1, 8, width), lambda b, s, pt: (b, 0, 0)),
            pl.BlockSpec((page, page), lambda b, s, pt: (0, 0)),
            pl.BlockSpec((grp * page, grp * page), lambda b, s, pt: (0, 0)),
        ] + [page_spec(u) for u in range(ppb)] + [page_spec(u) for u in range(ppb)],
        out_specs=pl.BlockSpec((1, seq, width), lambda b, s, pt: (b, 0, 0)),
        scratch_shapes=[pltpu.VMEM((nq, width), F32), pltpu.VMEM((1, nq), F32)]
        + [pltpu.VMEM((grp * page, width), BF16)] * (2 * (ppb // grp)))
    return pl.pallas_call(
        kern, grid_spec=grid_spec,
        out_shape=jax.ShapeDtypeStruct((n_seq, seq, width), BF16),
        compiler_params=_cparams(("arbitrary", "arbitrary"), 52), name="sb_sample")(
            page_table, qbd, brow, pad(k_s), pad(v_s), tn, tp,
            *([cache_k] * ppb), *([cache_v] * ppb))


def _top16_rows(s):
    n = s.shape[0]
    rid = lax.broadcasted_iota(jnp.int32, s.shape, 0)
    out = []
    for _ in range(PEER_K):
        m = jnp.max(s, axis=0, keepdims=True)
        first = jnp.min(jnp.where(s == m, rid, n), axis=0, keepdims=True)
        out.append(m)
        s = jnp.where(rid == first, -jnp.inf, s)
    return out


def _cmpx(v, i, j):
    a, b = v[i], v[j]
    v[i], v[j] = jnp.maximum(a, b), jnp.minimum(a, b)


def _bitonic_merge_desc(v):
    j = len(v) // 2
    while j >= 1:
        for i in range(len(v)):
            if i ^ j > i:
                _cmpx(v, i, i ^ j)
        j //= 2


def _bitonic_sort_desc(v):
    k = 2
    while k <= len(v):
        j = k // 2
        while j >= 1:
            for i in range(len(v)):
                if i ^ j > i:
                    _cmpx(v, *((i, i ^ j) if (i & k) == 0 else (i ^ j, i)))
            j //= 2
        k *= 2


def _top16_sorted(s):
    v = [s[8 * r:8 * r + 8, :] for r in range(PEER_K)]
    _bitonic_sort_desc(v)
    for shift in (4, 2, 1):
        w = [pltpu.roll(x, shift, 0) for x in v]
        v = [jnp.maximum(v[i], w[PEER_K - 1 - i]) for i in range(PEER_K)]
        _bitonic_merge_desc(v)
    return [x[0:1, :] for x in v]


def _peer_topk_kernel(q_ref, keys_ref, s1_ref, s2_ref, st_ref, *, n_heads):
    tm = q_ref.shape[0]
    for h in range(n_heads):
        tops = []
        for c in range(2):
            qh = q_ref[:, (2 * h + c) * 128:(2 * h + c + 1) * 128]
            sc = _dot_nt(keys_ref[h, c].astype(BF16), qh)
            (s1_ref, s2_ref)[c][h] = sc
            tops.append(_top16_sorted(sc))
        sv1 = jnp.concatenate(tops[0], axis=0)
        sv2 = jnp.concatenate(tops[1], axis=0)
        cand = jnp.concatenate(
            [tops[0][0] + sv2] + [tops[0][a] + sv2[0:8] for a in range(1, 8)]
            + [sv1[8:16] + tops[1][0]], axis=0)
        best = _top16_rows(cand)
        m = best[0]
        zsum = sum(jnp.exp(v - m) for v in best)
        max2 = tops[1][0]
        st_ref[h] = jnp.concatenate(
            [best[PEER_K - 1], m + jnp.log(zsum) - max2, max2, jnp.zeros((5, tm), F32)], axis=0)


def peer_topk(qp, sub_keys, tm):
    r = qp.shape[0]
    n_heads = sub_keys.shape[0]
    kern = functools.partial(_peer_topk_kernel, n_heads=n_heads)
    return pl.pallas_call(
        kern, grid=(r // tm,),
        in_specs=[pl.BlockSpec((tm, qp.shape[1]), lambda i: (i, 0)),
                  pl.BlockSpec(sub_keys.shape, lambda i: (0, 0, 0, 0))],
        out_specs=[pl.BlockSpec((n_heads, PEER_KEYS, tm), lambda i: (0, 0, i)),
                   pl.BlockSpec((n_heads, PEER_KEYS, tm), lambda i: (0, 0, i)),
                   pl.BlockSpec((n_heads, 8, tm), lambda i: (0, 0, i))],
        out_shape=[jax.ShapeDtypeStruct((n_heads, PEER_KEYS, r), F32),
                   jax.ShapeDtypeStruct((n_heads, PEER_KEYS, r), F32),
                   jax.ShapeDtypeStruct((n_heads, 8, r), F32)],
        compiler_params=_cparams(("parallel",), 40), name="peer_topk")(qp, sub_keys)


def _peer_dense_kernel(hn_ref, u_ref, vlo_ref, vhi_ref, s1_ref, s2_ref, st_ref, o_ref, p_prev, e2_s,
                       *, n_heads, half, n_trips):
    k = pl.program_id(1)

    @pl.when(k == 0)
    def _():
        o_ref[...] = jnp.zeros_like(o_ref)
        p_prev[...] = jnp.zeros_like(p_prev)
        for h in range(n_heads):
            e2_s[h] = jnp.exp(s2_ref[h] - st_ref[h, 2:3, :])

    live = (k < n_trips - 1).astype(F32)
    hn = hn_ref[...]

    hids = [_dot_nt(u_ref[part * half:(part + 1) * half, :], hn) for part in range(2)]
    first = _dot_tn(p_prev[...], vlo_ref[...])

    def gate_act(part):
        hid = hids[part]
        act = (0.5 * live) * hid * (1.0 + lax.erf(hid * (2.0 ** -0.5)))
        out = []
        for ii in range(half // PEER_KEYS):
            i_row = part * (half // PEER_KEYS) + ii
            wsum = None
            for h in range(n_heads):
                s1 = s1_ref[i_row, h:h + 1, :]
                sel = (s1 + s2_ref[h]) >= st_ref[h, 0:1, :]
                g = jnp.where(sel, jnp.exp(s1 - st_ref[h, 1:2, :]) * e2_s[h], 0.0)
                wsum = g if wsum is None else wsum + g
            out.append((wsum * act[ii * PEER_KEYS:(ii + 1) * PEER_KEYS, :]).astype(BF16))
        return jnp.concatenate(out, axis=0)

    pa = gate_act(0)
    second = _dot_tn(pa, vhi_ref[...])
    p_prev[...] = gate_act(1)
    o_ref[...] += first + second


def peer_dense(hn, eu, ev, s1, s2, st, tm, half):
    r, d = hn.shape
    n_exp = eu.shape[0]
    n_heads = s2.shape[0]
    n_pairs = n_exp // (2 * half)
    n_half = n_exp // half
    kern = functools.partial(_peer_dense_kernel, n_heads=n_heads, half=half, n_trips=n_pairs + 1)
    once = pl.Buffered(1)
    return pl.pallas_call(
        kern, grid=(pl.cdiv(r, tm), n_pairs + 1),
        in_specs=[pl.BlockSpec((tm, d), lambda i, k: (i, 0), pipeline_mode=once),
                  pl.BlockSpec((2 * half, d), lambda i, k: (jnp.minimum(k, n_pairs - 1), 0)),
                  pl.BlockSpec((half, d), lambda i, k: (jnp.maximum(2 * k - 1, 0), 0)),
                  pl.BlockSpec((half, d), lambda i, k: (jnp.minimum(2 * k, n_half - 1), 0)),
                  pl.BlockSpec((2 * half // PEER_KEYS, n_heads, tm),
                               lambda i, k: (jnp.minimum(k, n_pairs - 1), 0, i)),
                  pl.BlockSpec((n_heads, PEER_KEYS, tm), lambda i, k: (0, 0, i), pipeline_mode=once),
                  pl.BlockSpec((n_heads, 8, tm), lambda i, k: (0, 0, i), pipeline_mode=once)],
        out_specs=pl.BlockSpec((tm, d), lambda i, k: (i, 0), pipeline_mode=once),
        out_shape=jax.ShapeDtypeStruct((r, d), F32),
        scratch_shapes=[pltpu.VMEM((half, tm), BF16), pltpu.VMEM((n_heads, PEER_KEYS, tm), F32)],
        compiler_params=_cparams(("parallel", "arbitrary"), 56), name="peer_dense")(
            hn, eu, ev, ev, s1, s2, st)


def kernel(x_prompt, x_sample, cache_k, cache_v, state_ssm, state_conv, page_table, norm_mix, w_in,
           conv_w, conv_b, dt_bias, a_log, d_skip, ssm_norm_w, sb_bias, w_branch_a, w_branch_b,
           w_out, norm_ffn, w_query, sub_keys, expert_u, expert_v, norm_final):
    depth = w_in.shape[0]
    assert depth == 1 and x_prompt.shape[0] == 1
    _, t, d = x_prompt.shape
    n_seq, seq, _ = x_sample.shape
    n_tok = n_seq * seq
    r = t + n_tok
    n_ssm_heads = dt_bias.shape[1]
    d_inner = n_ssm_heads * HEAD_P
    g = n_ssm_heads // GROUP_HEADS
    conv_dim = conv_w.shape[2]
    n_sb = sb_bias.shape[1]
    sbw = n_sb * SB_DH
    assert t % 256 == 0 and n_tok == 128 and r % 640 == 0

    wt = jnp.transpose(w_in[0])
    dt0 = d_inner + conv_dim
    w_rest = wt[dt0 + n_ssm_heads:].astype(BF16)
    w_dt = jnp.pad(wt[dt0:dt0 + n_ssm_heads], ((0, 128 - n_ssm_heads), (0, 0)))

    x_all = jnp.concatenate([x_prompt[0], x_sample.reshape(n_tok, d)], axis=0)
    xn = rmsnorm(x_all, norm_mix[0], BF16)
    tm5 = r // 5
    (proj,) = matmul(xn, wt, [F32], tm5, 256, col0=0, n=dt0, w_rows=True)
    qscale = SB_DH ** -0.5 * LOG2E
    tm_p = max(c for c in range(128, 1025, 128) if t % c == 0)
    prm = dict(tm=tm_p, tn=512, row0=0, m=t, w_rows=True)
    smp = dict(tm=n_tok, tn=512, row0=t, m=n_tok, w_rows=True)
    (q_p,) = matmul(xn, w_rest, [BF16], col0=0, n=sbw, scale=qscale, **prm)
    (q_s,) = matmul(xn, w_rest, [BF16], col0=0, n=sbw, scale=qscale, **smp)
    kp_f, kp_bf = matmul(xn, w_rest, [F32, BF16], col0=sbw, n=sbw, **prm)
    (k_s,) = matmul(xn, w_rest, [F32], col0=sbw, n=sbw, **smp)
    vp_f, vp_bf = matmul(xn, w_rest, [F32, BF16], col0=2 * sbw, n=sbw, **prm)
    (v_s,) = matmul(xn, w_rest, [F32], col0=2 * sbw, n=sbw, **smp)
    (gates,) = matmul(xn, w_rest, [F32], tm5, 512, col0=3 * sbw, n=2 * d, w_rows=True)
    dt_raw = matmul(xn, w_dt, [F32], tm5, 128, w_rows=True)[0][:, :n_ssm_heads]

    def head_layouts(v):
        rows = v.shape[0]
        vg = v.reshape(rows, g, GROUP_HEADS).transpose(1, 0, 2)
        nat = jnp.pad(vg, ((0, 0), (0, 0), (0, 128 - GROUP_HEADS)))
        tr = jnp.pad(vg.transpose(0, 2, 1), ((0, 0), (0, 16 - GROUP_HEADS), (0, 0)))
        return nat, tr

    hp = jnp.stack([dt_bias[0], a_log[0], d_skip[0]], axis=0)
    hp_g = hp.reshape(3, g, GROUP_HEADS).transpose(1, 0, 2)
    hpn = jnp.pad(hp_g, ((0, 0), (0, 5), (0, 128 - GROUP_HEADS)))
    hpt = jnp.pad(hp_g.transpose(0, 2, 1), ((0, 0), (0, 16 - GROUP_HEADS), (0, 125)))
    nw = ssm_norm_w[0].reshape(1, d_inner)
    cw, cbias = conv_w[0], conv_b[0].reshape(1, conv_dim)

    dtn_p, dtt_p = head_layouts(dt_raw[:t])
    ya_head, ssm_p = ssd_prompt(proj, t, dtn_p, dtt_p, cw, cbias, hpn, hpt, nw, L=256)

    xbc_s = proj[t:, d_inner:d_inner + conv_dim].reshape(n_seq, seq, conv_dim)
    ext_s = jnp.concatenate([state_conv[0], xbc_s], axis=1)
    sh = jnp.stack([ext_s[:, k:k + seq].reshape(n_tok, conv_dim) for k in range(4)], axis=0)
    dtn_s, dtt_s = head_layouts(dt_raw[t:])
    ya, ssm_s = ssd_sample(sh, proj, t // n_tok, dtn_s, dtt_s, cw, cbias, hpn, hpt, nw,
                           state_ssm[0].reshape(n_seq, d_inner, D_STATE), seq, ya_head)

    yb_p = sb_prompt(q_p, kp_bf, vp_bf, t, sb_bias[0], blk=256, hb=8)
    n_phys, page = cache_k.shape[1], cache_k.shape[2]
    yb_s = sb_sample(q_s, k_s, v_s, cache_k.reshape(n_phys, page * n_sb, SB_DH),
                     cache_v.reshape(n_phys, page * n_sb, SB_DH), page_table, sb_bias[0], n_sb, seq)

    yb = jnp.concatenate([yb_p, yb_s.reshape(n_tok, sbw)], axis=0)
    merged = merge_branches(ya, yb, w_branch_a[0].astype(BF16), w_branch_b[0].astype(BF16),
                            gates, 0, d, tm=640, tn=512)
    h1 = matmul_resid(merged, w_out[0].astype(BF16), x_all, tm=640, tn=512)

    hn = rmsnorm(h1, norm_ffn[0], BF16)
    (qp,) = matmul(hn, w_query[0].astype(BF16), [BF16], tm5, 512)
    s1_hk, s2, stats = peer_topk(qp, sub_keys[0], tm=128)
    s1 = s1_hk.transpose(1, 0, 2)
    peer = peer_dense(hn, expert_u[0].astype(BF16), expert_v[0].astype(BF16), s1, s2, stats,
                      tm=768 if r > 768 else r, half=256)
    y_p, y_s = rmsnorm_add_split(h1, peer, norm_final, t)

    y_prompt = y_p.reshape(1, t, d)
    y_sample = y_s.reshape(n_seq, seq, d)
    k_prompt = kp_f.reshape(1, 1, t, n_sb, SB_DH)
    v_prompt = vp_f.reshape(1, 1, t, n_sb, SB_DH)
    ssm_prompt = ssm_p.reshape(1, 1, n_ssm_heads, HEAD_P, D_STATE)
    conv_prompt = proj[t - 3:t, d_inner:d_inner + conv_dim].reshape(1, 1, 3, conv_dim)
    k_sample = k_s.reshape(1, n_seq, seq, n_sb, SB_DH)
    v_sample = v_s.reshape(1, n_seq, seq, n_sb, SB_DH)
    ssm_sample = ssm_s.reshape(1, n_seq, n_ssm_heads, HEAD_P, D_STATE)
    conv_sample = ext_s[:, seq:].reshape(1, n_seq, 3, conv_dim)
    return (y_prompt, y_sample, k_prompt, v_prompt, ssm_prompt, conv_prompt,
            k_sample, v_sample, ssm_sample, conv_sample)
```

```python
import functools
import math

import numpy as np
import jax
import jax.numpy as jnp
from jax import lax
from jax.experimental import pallas as pl
from jax.experimental.pallas import tpu as pltpu

F32 = jnp.float32
BF16 = jnp.bfloat16
EPS = 1e-6
NEG_BIG = -1e30

HEAD_P = 64
GROUP_HEADS = 8
GROUP_W = HEAD_P * GROUP_HEADS
D_STATE = 128
SB_DH = 128
LAG = 1
PEER_K = 16
PEER_KEYS = 128


def _cparams(sem, vmem_mb):
    return pltpu.CompilerParams(dimension_semantics=sem,
                                vmem_limit_bytes=vmem_mb * 1024 * 1024)


def _softplus(x):
    return jnp.maximum(x, 0.0) + jnp.log1p(jnp.exp(-jnp.abs(x)))


LOG2E = 1.4426950408889634


def _softplus2(x):
    return jnp.where(x > 64.0, x, jnp.log(1.0 + jnp.exp2(x)) * LOG2E)


def _silu(x):
    return x * jax.nn.sigmoid(x)


def _split2(x):
    hi = x.astype(BF16)
    lo = (x - hi.astype(F32)).astype(BF16)
    return hi, lo


def _split3(x):
    hi = x.astype(BF16)
    r = x - hi.astype(F32)
    mid = r.astype(BF16)
    lo = (r - mid.astype(F32)).astype(BF16)
    return hi, mid, lo


def _dot(a, b):
    return jnp.dot(a, b, preferred_element_type=F32)


def _dot_nt(a, b):
    return lax.dot_general(a, b, (((1,), (1,)), ((), ())), preferred_element_type=F32)


def _dot_tn(a, b):
    return lax.dot_general(a, b, (((0,), (0,)), ((), ())), preferred_element_type=F32)


def _rms_kernel(x_ref, w_ref, o_ref):
    x = x_ref[...]
    r = x * lax.rsqrt(jnp.mean(x * x, axis=-1, keepdims=True) + EPS)
    o_ref[...] = (r * w_ref[...]).astype(o_ref.dtype)


def rmsnorm(x, w, out_dtype, tm=128):
    m, d = x.shape
    row = pl.BlockSpec((tm, d), lambda i: (i, 0))
    return pl.pallas_call(
        _rms_kernel, grid=(m // tm,),
        in_specs=[row, pl.BlockSpec((1, d), lambda i: (0, 0))], out_specs=row,
        out_shape=jax.ShapeDtypeStruct((m, d), out_dtype),
        compiler_params=_cparams(("parallel",), 40), name="rmsnorm")(x, w.reshape(1, d))


def _rms_add_split_kernel(x_ref, y_ref, w_ref, head_ref, tail_ref, *, head_blocks):
    i = pl.program_id(0)
    x = x_ref[...] + y_ref[...]
    r = x * lax.rsqrt(jnp.mean(x * x, axis=-1, keepdims=True) + EPS)
    out = r * w_ref[...]

    @pl.when(i < head_blocks)
    def _():
        head_ref[...] = out

    @pl.when(i >= head_blocks)
    def _():
        tail_ref[...] = out


def rmsnorm_add_split(x, y, w, n_head, tm=128):
    m, d = x.shape
    hb = n_head // tm
    row = pl.BlockSpec((tm, d), lambda i: (i, 0))
    return pl.pallas_call(
        functools.partial(_rms_add_split_kernel, head_blocks=hb), grid=(m // tm,),
        in_specs=[row, row, pl.BlockSpec((1, d), lambda i: (0, 0))],
        out_specs=[pl.BlockSpec((tm, d), lambda i: (jnp.minimum(i, hb - 1), 0)),
                   pl.BlockSpec((tm, d), lambda i: (jnp.maximum(i - hb, 0), 0))],
        out_shape=[jax.ShapeDtypeStruct((n_head, d), F32),
                   jax.ShapeDtypeStruct((m - n_head, d), F32)],
        compiler_params=_cparams(("arbitrary",), 40), name="rmsnorm_out")(x, y, w.reshape(1, d))


def _mm_kernel(a_ref, w_ref, *o_refs, scale, w_rows):
    w = w_ref[...].astype(BF16)
    acc = _dot_nt(a_ref[...], w) if w_rows else _dot(a_ref[...], w)
    if scale is not None:
        acc = acc * scale
    for o_ref in o_refs:
        o_ref[...] = acc.astype(o_ref.dtype)


def matmul(a, w, out_dtypes, tm, tn, col0=0, n=None, scale=None, row0=0, m=None, w_rows=False):
    k = a.shape[1]
    m = a.shape[0] if m is None else m
    if n is None:
        n = w.shape[0] if w_rows else w.shape[1]
    c0, r0 = col0 // tn, row0 // tm
    out = pl.BlockSpec((tm, tn), lambda i, j: (i, j))
    wspec = (pl.BlockSpec((tn, k), lambda i, j: (c0 + j, 0)) if w_rows
             else pl.BlockSpec((k, tn), lambda i, j: (0, c0 + j)))
    return pl.pallas_call(
        functools.partial(_mm_kernel, scale=scale, w_rows=w_rows), grid=(m // tm, n // tn),
        in_specs=[pl.BlockSpec((tm, k), lambda i, j: (r0 + i, 0)), wspec],
        out_specs=[out] * len(out_dtypes),
        out_shape=[jax.ShapeDtypeStruct((m, n), dt) for dt in out_dtypes],
        compiler_params=_cparams(("parallel", "arbitrary"), 52), name="matmul")(a, w)


def _mm_resid_kernel(a_ref, w_ref, x_ref, o_ref):
    o_ref[...] = x_ref[...] + _dot(a_ref[...], w_ref[...])


def matmul_resid(a, w, x, tm, tn):
    m, k = a.shape
    n = w.shape[1]
    return pl.pallas_call(
        _mm_resid_kernel, grid=(m // tm, n // tn),
        in_specs=[pl.BlockSpec((tm, k), lambda i, j: (i, 0)),
                  pl.BlockSpec((k, tn), lambda i, j: (0, j)),
                  pl.BlockSpec((tm, tn), lambda i, j: (i, j))],
        out_specs=pl.BlockSpec((tm, tn), lambda i, j: (i, j)),
        out_shape=jax.ShapeDtypeStruct((m, n), F32),
        compiler_params=_cparams(("parallel", "arbitrary"), 52), name="matmul_resid")(a, w, x)


def _mm_merge_kernel(ya_ref, yb_ref, wa_ref, wb_ref, ga_ref, gb_ref, o_ref):
    a = _dot(ya_ref[...], wa_ref[...])
    b = _dot(yb_ref[...], wb_ref[...])
    o_ref[...] = (jax.nn.sigmoid(ga_ref[...]) * a + jax.nn.sigmoid(gb_ref[...]) * b).astype(o_ref.dtype)


def merge_branches(ya, yb, wa, wb, proj, ga_col, gb_col, tm, tn):
    m, ka = ya.shape
    kb = yb.shape[1]
    n = wa.shape[1]
    ga0, gb0 = ga_col // tn, gb_col // tn
    return pl.pallas_call(
        _mm_merge_kernel, grid=(m // tm, n // tn),
        in_specs=[pl.BlockSpec((tm, ka), lambda i, j: (i, 0)),
                  pl.BlockSpec((tm, kb), lambda i, j: (i, 0)),
                  pl.BlockSpec((ka, tn), lambda i, j: (0, j)),
                  pl.BlockSpec((kb, tn), lambda i, j: (0, j)),
                  pl.BlockSpec((tm, tn), lambda i, j: (i, ga0 + j)),
                  pl.BlockSpec((tm, tn), lambda i, j: (i, gb0 + j))],
        out_specs=pl.BlockSpec((tm, tn), lambda i, j: (i, j)),
        out_shape=jax.ShapeDtypeStruct((m, n), BF16),
        compiler_params=_cparams(("parallel", "arbitrary"), 52), name="merge")(
            ya, yb, wa, wb, proj, proj)


def _conv_silu(ext, w_ref, b_ref, L):
    acc = b_ref[...] + w_ref[3:4, :] * ext[8:8 + L, :]
    acc = acc + w_ref[2:3, :] * ext[7:7 + L, :]
    acc = acc + w_ref[1:2, :] * ext[6:6 + L, :]
    acc = acc + w_ref[0:1, :] * ext[5:5 + L, :]
    return _silu(acc)


def _pair_terms(pr, x, dt, acs, acs_t, cb, ch, mask, hpn, lo_half):
    ra, rb = 2 * pr, 2 * pr + 1
    n_rows = acs.shape[0]
    cola = jnp.broadcast_to(acs[:, ra:ra + 1], (n_rows, 128))
    colb = jnp.broadcast_to(acs[:, rb:rb + 1], (n_rows, 128))
    reps = mask.shape[1] // 128
    rowa, rowb = acs_t[ra:ra + 1, :], acs_t[rb:rb + 1, :]
    ma = cb * jnp.exp(jnp.where(mask, jnp.tile(cola, (1, reps)) - rowa, NEG_BIG))
    mb = cb * jnp.exp(jnp.where(mask, jnp.tile(colb, (1, reps)) - rowb, NEG_BIG))
    mcat = jnp.concatenate([ma.astype(BF16), mb.astype(BF16)], axis=1)
    xp = x[:, pr * 128:(pr + 1) * 128]
    xdt = xp * jnp.where(lo_half, dt[:, ra:ra + 1], dt[:, rb:rb + 1])
    rhs = jnp.concatenate([jnp.where(lo_half, xdt, 0.0), jnp.where(lo_half, 0.0, xdt)],
                          axis=0).astype(BF16)
    yd = _dot(mcat, rhs)
    acs_pair = jnp.where(lo_half, cola, colb)
    dsk = jnp.where(lo_half[0:1, :], hpn[2:3, ra:ra + 1], hpn[2:3, rb:rb + 1])
    y = yd + ch[:, pr * 128:(pr + 1) * 128] * jnp.exp(acs_pair) + xp * dsk
    return y, xdt, acs_pair


def _ssd_prompt_kernel(xs_ref, b_ref, c_ref, z_ref, dtn_ref, dtt_ref,
                       cwx_ref, cwb_ref, cwc_ref, cbx_ref, cbb_ref, cbc_ref,
                       hpn_ref, hpt_ref, nw_ref, lin_ref, uin_ref,
                       y_ref, hout_ref, extx, extb, extc, hs, *, L):
    c = pl.program_id(1)

    @pl.when(c == 0)
    def _():
        extx[0:8, :] = jnp.zeros((8, GROUP_W), F32)
        extb[0:8, :] = jnp.zeros((8, D_STATE), F32)
        extc[0:8, :] = jnp.zeros((8, D_STATE), F32)
        hs[...] = jnp.zeros_like(hs)

    extx[8:8 + L, :] = xs_ref[...]
    extb[8:8 + L, :] = b_ref[...]
    extc[8:8 + L, :] = c_ref[...]
    x = _conv_silu(extx, cwx_ref, cbx_ref, L)
    bm = _conv_silu(extb, cwb_ref, cbb_ref, L).astype(BF16)
    cm = _conv_silu(extc, cwc_ref, cbc_ref, L).astype(BF16)
    extx[0:8, :] = extx[L:L + 8, :]
    extb[0:8, :] = extb[L:L + 8, :]
    extc[0:8, :] = extc[L:L + 8, :]

    hpn = hpn_ref[0]
    hpt = hpt_ref[0]
    dt = _softplus(dtn_ref[0] + hpn[0:1, :])
    d_a = dt * (-jnp.exp(hpn[1:2, :]))
    d_at = _softplus(dtt_ref[0] + hpt[:, 0:1]) * (-jnp.exp(hpt[:, 1:2]))
    lin = lin_ref[...]
    uin = uin_ref[...]
    acs = sum(_dot(lin, p) for p in _split3(d_a))
    acs_t = sum(_dot(p, uin) for p in _split3(d_at))

    rows = lax.broadcasted_iota(jnp.int32, (L, L), 0)
    cols = lax.broadcasted_iota(jnp.int32, (L, L), 1)
    causal = rows >= cols
    lo_half = lax.broadcasted_iota(jnp.int32, (L, 128), 1) < HEAD_P
    sub_lo = lax.broadcasted_iota(jnp.int32, (128, D_STATE), 0) < HEAD_P

    cb = _dot_nt(cm, bm)
    ch = _dot_nt(cm, hs[...].astype(BF16))

    ys = []
    for pr in range(GROUP_HEADS // 2):
        ra, rb = 2 * pr, 2 * pr + 1
        y, xdt, acs_pair = _pair_terms(pr, x, dt, acs, acs_t, cb, ch, causal, hpn, lo_half)
        ys.append(y)
        lasta = acs_t[ra:ra + 1, L - 1:L]
        lastb = acs_t[rb:rb + 1, L - 1:L]
        dend = jnp.exp(jnp.where(lo_half[0:1, :], lasta, lastb) - acs_pair)
        s_new = _dot_tn((xdt * dend).astype(BF16), bm)
        cd = jnp.where(sub_lo, jnp.exp(lasta), jnp.exp(lastb))
        sl = slice(pr * 128, (pr + 1) * 128)
        hs[sl, :] = hs[sl, :] * cd + s_new

    yg = jnp.concatenate(ys, axis=1)
    u = yg * _silu(z_ref[...])
    u = u * lax.rsqrt(jnp.mean(u * u, axis=-1, keepdims=True) + EPS)
    y_ref[...] = (u * nw_ref[...]).astype(y_ref.dtype)

    @pl.when(c == pl.num_programs(1) - 1)
    def _():
        hout_ref[...] = hs[...]


def _tri_incl(n):
    return np.tril(np.ones((n, n), np.float32))


def ssd_prompt(proj, t, dt_n, dt_t, conv_w, conv_b, hpn, hpt, norm_w, L):
    g = dt_n.shape[0]
    nc = t // L
    d_inner = g * GROUP_W
    xb = d_inner // GROUP_W
    bb = (2 * d_inner) // D_STATE
    cbk = bb + g
    lin = jnp.asarray(_tri_incl(L), BF16)
    uin = jnp.asarray(_tri_incl(L).T, BF16)
    const = lambda gi, ci: (0, 0)
    kern = functools.partial(_ssd_prompt_kernel, L=L)
    return pl.pallas_call(
        kern, grid=(g, nc),
        in_specs=[
            pl.BlockSpec((L, GROUP_W), lambda gi, ci: (ci, xb + gi)),
            pl.BlockSpec((L, D_STATE), lambda gi, ci: (ci, bb + gi)),
            pl.BlockSpec((L, D_STATE), lambda gi, ci: (ci, cbk + gi)),
            pl.BlockSpec((L, GROUP_W), lambda gi, ci: (ci, gi)),
            pl.BlockSpec((1, L, 128), lambda gi, ci: (gi, ci, 0)),
            pl.BlockSpec((1, 16, L), lambda gi, ci: (gi, 0, ci)),
            pl.BlockSpec((4, GROUP_W), lambda gi, ci: (0, gi)),
            pl.BlockSpec((4, D_STATE), lambda gi, ci: (0, bb - xb * 4 + gi)),
            pl.BlockSpec((4, D_STATE), lambda gi, ci: (0, bb - xb * 4 + g + gi)),
            pl.BlockSpec((1, GROUP_W), lambda gi, ci: (0, gi)),
            pl.BlockSpec((1, D_STATE), lambda gi, ci: (0, bb - xb * 4 + gi)),
            pl.BlockSpec((1, D_STATE), lambda gi, ci: (0, bb - xb * 4 + g + gi)),
            pl.BlockSpec((1, 8, 128), lambda gi, ci: (gi, 0, 0)),
            pl.BlockSpec((1, 16, 128), lambda gi, ci: (gi, 0, 0)),
            pl.BlockSpec((1, GROUP_W), lambda gi, ci: (0, gi)),
            pl.BlockSpec((L, L), const),
            pl.BlockSpec((L, L), const),
        ],
        out_specs=[pl.BlockSpec((L, GROUP_W), lambda gi, ci: (ci, gi)),
                   pl.BlockSpec((GROUP_W, D_STATE), lambda gi, ci: (gi, 0))],
        out_shape=[jax.ShapeDtypeStruct((proj.shape[0], d_inner), BF16),
                   jax.ShapeDtypeStruct((d_inner, D_STATE), F32)],
        scratch_shapes=[pltpu.VMEM((L + 8, GROUP_W), F32), pltpu.VMEM((L + 8, D_STATE), F32),
                        pltpu.VMEM((L + 8, D_STATE), F32), pltpu.VMEM((GROUP_W, D_STATE), F32)],
        compiler_params=_cparams(("arbitrary", "arbitrary"), 40), name="ssd_prompt")(
            proj, proj, proj, proj, dt_n, dt_t, conv_w, conv_w, conv_w, conv_b, conv_b, conv_b,
            hpn, hpt, norm_w, lin, uin)


def _ssd_sample_kernel(shx_ref, shb_ref, shc_ref, z_ref, dtn_ref, dtt_ref,
                       cwx_ref, cwb_ref, cwc_ref, cbx_ref, cbb_ref, cbc_ref,
                       hpn_ref, hpt_ref, nw_ref, lseg_ref, useg_ref, sseg_ref, st_ref, y_all_ref,
                       y_ref, hout_ref, yacc, cm_s, bm_s, ea_s, xwt_s, cdl_s, *, n_tok, seq):
    b = pl.program_id(1)

    @pl.when(b == 0)
    def _():
        def conv(sh_ref, w_ref, b_ref):
            acc = b_ref[...] + w_ref[0:1, :] * sh_ref[0]
            for k in range(1, 4):
                acc = acc + w_ref[k:k + 1, :] * sh_ref[k]
            return _silu(acc)

        x = conv(shx_ref, cwx_ref, cbx_ref)
        bm = conv(shb_ref, cwb_ref, cbb_ref).astype(BF16)
        cm = conv(shc_ref, cwc_ref, cbc_ref).astype(BF16)
        hpn = hpn_ref[0]
        hpt = hpt_ref[0]
        dt = _softplus(dtn_ref[0] + hpn[0:1, :])
        d_a = dt * (-jnp.exp(hpn[1:2, :]))
        d_at = _softplus(dtt_ref[0] + hpt[:, 0:1]) * (-jnp.exp(hpt[:, 1:2]))
        lseg = lseg_ref[...]
        useg = useg_ref[...]
        sseg = sseg_ref[...]
        pa, pat = _split3(d_a), _split3(d_at)
        acs = sum(_dot(lseg, p) for p in pa)
        tot = sum(_dot(sseg, p) for p in pa)
        acs_t = sum(_dot(p, useg) for p in pat)
        tot_t = sum(_dot(p, sseg) for p in pat)
        mask = lseg > 0
        lo_half = lax.broadcasted_iota(jnp.int32, (n_tok, 128), 1) < HEAD_P
        cb = _dot_nt(cm, bm)
        zero_ch = jnp.zeros((n_tok, GROUP_W), F32)
        ys, xws, eas = [], [], []
        for pr in range(GROUP_HEADS // 2):
            ra, rb = 2 * pr, 2 * pr + 1
            y, xdt, acs_pair = _pair_terms(pr, x, dt, acs, acs_t, cb, zero_ch, mask, hpn, lo_half)
            tot_pair = jnp.where(lo_half, tot[:, ra:ra + 1], tot[:, rb:rb + 1])
            ys.append(y)
            xws.append(xdt * jnp.exp(tot_pair - acs_pair))
            eas.append(jnp.exp(acs_pair))
        yacc[...] = jnp.concatenate(ys, axis=1)
        ea_s[...] = jnp.concatenate(eas, axis=1)
        xwt_s[...] = jnp.transpose(jnp.concatenate(xws, axis=1)).astype(BF16)
        cm_s[...] = cm
        bm_s[...] = bm
        cdl_s[...] = jnp.concatenate(
            [jnp.broadcast_to(tot_t[r:r + 1, :], (HEAD_P, n_tok)) for r in range(GROUP_HEADS)], axis=0)

    h0 = st_ref[0]
    lo = b * seq
    rid = lax.broadcasted_iota(jnp.int32, (n_tok, GROUP_W), 0)
    in_rows = jnp.abs(2 * (rid - lo) - (seq - 1)) < seq
    lid = lax.broadcasted_iota(jnp.int32, (GROUP_W, n_tok), 1)
    in_lanes = jnp.abs(2 * (lid - lo) - (seq - 1)) < seq
    ch = _dot_nt(cm_s[...], h0.astype(BF16))
    yacc[...] += jnp.where(in_rows, ch * ea_s[...], 0.0)
    s_new = _dot(jnp.where(in_lanes, xwt_s[...], jnp.zeros_like(xwt_s[...])), bm_s[...])
    cdcol = jnp.sum(jnp.where(lid == lo, cdl_s[...], 0.0), axis=-1, keepdims=True)
    hout_ref[0] = h0 * jnp.exp(cdcol) + s_new

    @pl.when(b == pl.num_programs(1) - 1)
    def _():
        u = yacc[...] * _silu(z_ref[...])
        u = u * lax.rsqrt(jnp.mean(u * u, axis=-1, keepdims=True) + EPS)
        y_ref[...] = (u * nw_ref[...]).astype(y_ref.dtype)


def ssd_sample(sh, proj, z_row_block, dt_n, dt_t, conv_w, conv_b, hpn, hpt, norm_w, state, seq,
               y_all):
    g = dt_n.shape[0]
    n_tok = sh.shape[1]
    n_seq = n_tok // seq
    d_inner = g * GROUP_W
    xb4 = d_inner // D_STATE
    sid = np.arange(n_tok) // seq
    same = (sid[:, None] == sid[None, :]).astype(np.float32)
    lseg = jnp.asarray(same * _tri_incl(n_tok), BF16)
    useg = jnp.asarray((same * _tri_incl(n_tok)).T, BF16)
    sseg = jnp.asarray(same, BF16)
    const = lambda gi, bi: (0, 0)
    kern = functools.partial(_ssd_sample_kernel, n_tok=n_tok, seq=seq)
    return pl.pallas_call(
        kern, grid=(g, n_seq),
        in_specs=[
            pl.BlockSpec((4, n_tok, GROUP_W), lambda gi, bi: (0, 0, gi)),
            pl.BlockSpec((4, n_tok, D_STATE), lambda gi, bi: (0, 0, xb4 + gi)),
            pl.BlockSpec((4, n_tok, D_STATE), lambda gi, bi: (0, 0, xb4 + g + gi)),
            pl.BlockSpec((n_tok, GROUP_W), lambda gi, bi: (z_row_block, gi)),
            pl.BlockSpec((1, n_tok, 128), lambda gi, bi: (gi, 0, 0)),
            pl.BlockSpec((1, 16, n_tok), lambda gi, bi: (gi, 0, 0)),
            pl.BlockSpec((4, GROUP_W), lambda gi, bi: (0, gi)),
            pl.BlockSpec((4, D_STATE), lambda gi, bi: (0, xb4 + gi)),
            pl.BlockSpec((4, D_STATE), lambda gi, bi: (0, xb4 + g + gi)),
            pl.BlockSpec((1, GROUP_W), lambda gi, bi: (0, gi)),
            pl.BlockSpec((1, D_STATE), lambda gi, bi: (0, xb4 + gi)),
            pl.BlockSpec((1, D_STATE), lambda gi, bi: (0, xb4 + g + gi)),
            pl.BlockSpec((1, 8, 128), lambda gi, bi: (gi, 0, 0)),
            pl.BlockSpec((1, 16, 128), lambda gi, bi: (gi, 0, 0)),
            pl.BlockSpec((1, GROUP_W), lambda gi, bi: (0, gi)),
            pl.BlockSpec((n_tok, n_tok), const),
            pl.BlockSpec((n_tok, n_tok), const),
            pl.BlockSpec((n_tok, n_tok), const),
            pl.BlockSpec((1, GROUP_W, D_STATE), lambda gi, bi: (bi, gi, 0)),
            pl.BlockSpec(memory_space=pl.ANY),
        ],
        out_specs=[pl.BlockSpec((n_tok, GROUP_W), lambda gi, bi: (z_row_block, gi)),
                   pl.BlockSpec((1, GROUP_W, D_STATE), lambda gi, bi: (bi, gi, 0))],
        out_shape=[jax.ShapeDtypeStruct(y_all.shape, BF16),
                   jax.ShapeDtypeStruct(state.shape, F32)],
        input_output_aliases={19: 0},
        scratch_shapes=[pltpu.VMEM((n_tok, GROUP_W), F32), pltpu.VMEM((n_tok, D_STATE), BF16),
                        pltpu.VMEM((n_tok, D_STATE), BF16), pltpu.VMEM((n_tok, GROUP_W), F32),
                        pltpu.VMEM((GROUP_W, n_tok), BF16), pltpu.VMEM((GROUP_W, n_tok), F32)],
        compiler_params=_cparams(("arbitrary", "arbitrary"), 40), name="ssd_sample")(
            sh, sh, sh, proj, dt_n, dt_t, conv_w, conv_w, conv_w, conv_b, conv_b, conv_b,
            hpn, hpt, norm_w, lseg, useg, sseg, state, y_all)


def _sb_prompt_kernel(bias_ref, q_ref, k_ref, v_ref, u_ref, o_ref, *, blk, hb):
    g = pl.program_id(0)
    i = pl.program_id(1)
    umat = u_ref[...]
    rows = lax.broadcasted_iota(jnp.int32, (blk, blk), 0)
    cols = lax.broadcasted_iota(jnp.int32, (blk, blk), 1)
    strict = cols < rows

    def block(j, carry, masked):
        start = pl.multiple_of(j * blk, blk)
        hsl = [slice(h * SB_DH, (h + 1) * SB_DH) for h in range(hb)]
        zs, sps, sufs = [None] * hb, [None] * hb, [None] * hb
        accs, runs = [None] * hb, [None] * hb
        for slot in range(hb + 2 * LAG):
            h = slot
            if h < hb:
                zs[h] = (_dot_nt(q_ref[:, hsl[h]], k_ref[pl.ds(start, blk), hsl[h]])
                         + bias_ref[g * hb + h] * LOG2E)
                sp = _softplus2(zs[h])
                sps[h] = jnp.where(strict, sp, 0.0) if masked else sp
            h = slot - LAG
            if 0 <= h < hb:
                sufs[h] = _dot(sps[h].astype(BF16), umat)
            h = slot - 2 * LAG
            if 0 <= h < hb:
                w = jnp.exp2(zs[h] - sps[h] - sufs[h] - carry[h][1])
                if masked:
                    w = jnp.where(strict, w, 0.0)
                accs[h] = carry[h][0] + _dot(w.astype(BF16), v_ref[pl.ds(start, blk), hsl[h]])
                runs[h] = carry[h][1] + jnp.sum(sps[h], axis=-1, keepdims=True)
        return tuple(zip(accs, runs))

    carry = tuple((jnp.zeros((blk, SB_DH), F32), jnp.zeros((blk, 1), F32)) for _ in range(hb))
    carry = block(i, carry, True)
    carry = lax.fori_loop(0, i, lambda t, c: block(i - 1 - t, c, False), carry)
    for h in range(hb):
        o_ref[:, h * SB_DH:(h + 1) * SB_DH] = carry[h][0].astype(o_ref.dtype)


def _tri_later(n):
    return np.tril(np.ones((n, n), np.float32), -1)


def sb_prompt(q, k, v, t, bias, blk, hb):
    width = q.shape[1]
    gw = hb * SB_DH
    umat = jnp.asarray(_tri_later(blk), BF16)
    kern = functools.partial(_sb_prompt_kernel, blk=blk, hb=hb)
    return pl.pallas_call(
        kern, grid=(width // gw, t // blk),
        in_specs=[
            pl.BlockSpec(memory_space=pltpu.SMEM),
            pl.BlockSpec((blk, gw), lambda g, i: (i, g)),
            pl.BlockSpec((t, gw), lambda g, i: (0, g), pipeline_mode=pl.Buffered(1)),
            pl.BlockSpec((t, gw), lambda g, i: (0, g), pipeline_mode=pl.Buffered(1)),
            pl.BlockSpec((blk, blk), lambda g, i: (0, 0)),
        ],
        out_specs=pl.BlockSpec((blk, gw), lambda g, i: (i, g)),
        out_shape=jax.ShapeDtypeStruct((t, width), BF16),
        compiler_params=_cparams(("arbitrary", "arbitrary"), 48), name="sb_prompt")(
            bias, q, k, v, umat)


def _sb_sample_kernel(pt_ref, qbd_ref, brow_ref, kn_ref, vn_ref, tn_ref, tp_ref, *rest,
                      n_heads, seq, page, ppb, grp):
    k_pages, v_pages = rest[:ppb], rest[ppb:2 * ppb]
    o_ref, acc, run = rest[2 * ppb:2 * ppb + 3]
    bufs = rest[2 * ppb + 3:]
    s = pl.program_id(1)
    qbd = qbd_ref[0]
    brow = brow_ref[...]
    nq = n_heads * 8

    def process(kb, vb, tmat, mask):
        z = _dot(kb, qbd) + brow
        sp = _softplus2(z)
        if mask is not None:
            sp = jnp.where(mask, sp, 0.0)
        hi, lo = _split2(sp)
        suffix = _dot(tmat, hi) + _dot(tmat, lo)
        w = jnp.exp2(z - sp - suffix - run[...])
        if mask is not None:
            w = jnp.where(mask, w, 0.0)
        acc[...] += _dot_tn(w.astype(BF16), vb)
        run[...] += jnp.sum(sp, axis=0, keepdims=True)

    @pl.when(s == 0)
    def _():
        acc[...] = jnp.zeros_like(acc)
        run[...] = jnp.zeros_like(run)
        fill = jnp.zeros((page - 8, n_heads * SB_DH), F32)
        knew = jnp.concatenate([kn_ref[0], fill], axis=0).astype(BF16)
        vnew = jnp.concatenate([vn_ref[0], fill], axis=0).astype(BF16)
        key = lax.broadcasted_iota(jnp.int32, (page, nq), 0)
        qpos = lax.broadcasted_iota(jnp.int32, (page, nq), 1) % 8
        process(knew, vnew, tn_ref[...], key < qpos)

    for gi in range(ppb // grp):
        kc, vc = bufs[2 * gi], bufs[2 * gi + 1]
        for u in range(grp):
            kp, vp = k_pages[gi * grp + u], v_pages[gi * grp + u]
            r0 = (grp - 1 - u) * page
            for h in range(n_heads):
                hs = slice(h * SB_DH, (h + 1) * SB_DH)
                kc[r0:r0 + page, hs] = kp[0, pl.ds(h, page, stride=n_heads), :].astype(BF16)
                vc[r0:r0 + page, hs] = vp[0, pl.ds(h, page, stride=n_heads), :].astype(BF16)
        process(kc[...], vc[...], tp_ref[...], None)

    @pl.when(s == pl.num_programs(1) - 1)
    def _():
        for h in range(n_heads):
            o_ref[0, :, h * SB_DH:(h + 1) * SB_DH] = (
                acc[h * 8:h * 8 + seq, h * SB_DH:(h + 1) * SB_DH].astype(o_ref.dtype))


def sb_sample(q_s, k_s, v_s, cache_k, cache_v, page_table, bias, n_heads, seq):
    n_seq, n_pages = page_table.shape
    page = cache_k.shape[1] // n_heads
    width = n_heads * SB_DH
    nq = n_heads * 8
    grp = 4
    ppb = 8 if n_pages % 8 == 0 else 4
    steps = n_pages // ppb
    q4 = jnp.pad(q_s.reshape(n_seq, seq, n_heads, SB_DH), ((0, 0), (0, 8 - seq), (0, 0), (0, 0)))
    eye = jnp.eye(n_heads, dtype=q_s.dtype)
    qbd = jnp.einsum("bthd,hg->bhdgt", q4, eye).reshape(n_seq, width, nq).astype(BF16)
    brow = jnp.repeat(bias * LOG2E, 8).reshape(1, nq)
    pad = lambda a: jnp.pad(a.reshape(n_seq, seq, width), ((0, 0), (0, 8 - seq), (0, 0)))
    tn = jnp.asarray(_tri_later(page).T, BF16)
    tp = jnp.asarray(_tri_later(grp * page).T, BF16)

    def page_spec(u):
        return pl.BlockSpec((1, page * n_heads, SB_DH),
                            lambda b, s, pt: (pt[b, n_pages - 1 - (ppb * s + u)], 0, 0))

    kern = functools.partial(_sb_sample_kernel, n_heads=n_heads, seq=seq, page=page, ppb=ppb,
                             grp=grp)
    grid_spec = pltpu.PrefetchScalarGridSpec(
        num_scalar_prefetch=1, grid=(n_seq, steps),
        in_specs=[
            pl.BlockSpec((1, width, nq), lambda b, s, pt: (b, 0, 0)),
            pl.BlockSpec((1, nq), lambda b, s, pt: (0, 0)),
            pl.BlockSpec((1, 8, width), lambda b, s, pt: (b, 0, 0)),
            pl.BlockSpec((1, 8, width), lambda b, s, pt: (b, 0, 0)),
            pl.BlockSpec((page, page), lambda b, s, pt: (0, 0)),
            pl.BlockSpec((grp * page, grp * page), lambda b, s, pt: (0, 0)),
        ] + [page_spec(u) for u in range(ppb)] + [page_spec(u) for u in range(ppb)],
        out_specs=pl.BlockSpec((1, seq, width), lambda b, s, pt: (b, 0, 0)),
        scratch_shapes=[pltpu.VMEM((nq, width), F32), pltpu.VMEM((1, nq), F32)]
        + [pltpu.VMEM((grp * page, width), BF16)] * (2 * (ppb // grp)))
    return pl.pallas_call(
        kern, grid_spec=grid_spec,
        out_shape=jax.ShapeDtypeStruct((n_seq, seq, width), BF16),
        compiler_params=_cparams(("arbitrary", "arbitrary"), 52), name="sb_sample")(
            page_table, qbd, brow, pad(k_s), pad(v_s), tn, tp,
            *([cache_k] * ppb), *([cache_v] * ppb))


def _top16_rows(s):
    n = s.shape[0]
    rid = lax.broadcasted_iota(jnp.int32, s.shape, 0)
    out = []
    for _ in range(PEER_K):
        m = jnp.max(s, axis=0, keepdims=True)
        first = jnp.min(jnp.where(s == m, rid, n), axis=0, keepdims=True)
        out.append(m)
        s = jnp.where(rid == first, -jnp.inf, s)
    return out


def _cmpx(v, i, j):
    a, b = v[i], v[j]
    v[i], v[j] = jnp.maximum(a, b), jnp.minimum(a, b)


def _bitonic_merge_desc(v):
    j = len(v) // 2
    while j >= 1:
        for i in range(len(v)):
            if i ^ j > i:
                _cmpx(v, i, i ^ j)
        j //= 2


def _bitonic_sort_desc(v):
    k = 2
    while k <= len(v):
        j = k // 2
        while j >= 1:
            for i in range(len(v)):
                if i ^ j > i:
                    _cmpx(v, *((i, i ^ j) if (i & k) == 0 else (i ^ j, i)))
            j //= 2
        k *= 2


def _top16_sorted(s):
    v = [s[8 * r:8 * r + 8, :] for r in range(PEER_K)]
    _bitonic_sort_desc(v)
    for shift in (4, 2, 1):
        w = [pltpu.roll(x, shift, 0) for x in v]
        v = [jnp.maximum(v[i], w[PEER_K - 1 - i]) for i in range(PEER_K)]
        _bitonic_merge_desc(v)
    return [x[0:1, :] for x in v]


def _peer_topk_kernel(q_ref, keys_ref, s1_ref, s2_ref, st_ref, *, n_heads):
    tm = q_ref.shape[0]
    for h in range(n_heads):
        tops = []
        for c in range(2):
            qh = q_ref[:, (2 * h + c) * 128:(2 * h + c + 1) * 128]
            sc = _dot_nt(keys_ref[h, c].astype(BF16), qh)
            (s1_ref, s2_ref)[c][h] = sc
            tops.append(_top16_sorted(sc))
        sv1 = jnp.concatenate(tops[0], axis=0)
        sv2 = jnp.concatenate(tops[1], axis=0)
        cand = jnp.concatenate(
            [tops[0][0] + sv2] + [tops[0][a] + sv2[0:8] for a in range(1, 8)]
            + [sv1[8:16] + tops[1][0]], axis=0)
        best = _top16_rows(cand)
        m = best[0]
        zsum = sum(jnp.exp(v - m) for v in best)
        max2 = tops[1][0]
        st_ref[h] = jnp.concatenate(
            [best[PEER_K - 1], m + jnp.log(zsum) - max2, max2, jnp.zeros((5, tm), F32)], axis=0)


def peer_topk(qp, sub_keys, tm):
    r = qp.shape[0]
    n_heads = sub_keys.shape[0]
    kern = functools.partial(_peer_topk_kernel, n_heads=n_heads)
    return pl.pallas_call(
        kern, grid=(r // tm,),
        in_specs=[pl.BlockSpec((tm, qp.shape[1]), lambda i: (i, 0)),
                  pl.BlockSpec(sub_keys.shape, lambda i: (0, 0, 0, 0))],
        out_specs=[pl.BlockSpec((n_heads, PEER_KEYS, tm), lambda i: (0, 0, i)),
                   pl.BlockSpec((n_heads, PEER_KEYS, tm), lambda i: (0, 0, i)),
                   pl.BlockSpec((n_heads, 8, tm), lambda i: (0, 0, i))],
        out_shape=[jax.ShapeDtypeStruct((n_heads, PEER_KEYS, r), F32),
                   jax.ShapeDtypeStruct((n_heads, PEER_KEYS, r), F32),
                   jax.ShapeDtypeStruct((n_heads, 8, r), F32)],
        compiler_params=_cparams(("parallel",), 40), name="peer_topk")(qp, sub_keys)


def _peer_dense_kernel(hn_ref, u_ref, vlo_ref, vhi_ref, s1_ref, s2_ref, st_ref, o_ref, p_prev, e2_s,
                       *, n_heads, half, n_trips):
    k = pl.program_id(1)

    @pl.when(k == 0)
    def _():
        o_ref[...] = jnp.zeros_like(o_ref)
        p_prev[...] = jnp.zeros_like(p_prev)
        for h in range(n_heads):
            e2_s[h] = jnp.exp(s2_ref[h] - st_ref[h, 2:3, :])

    live = (k < n_trips - 1).astype(F32)
    hn = hn_ref[...]

    hids = [_dot_nt(u_ref[part * half:(part + 1) * half, :], hn) for part in range(2)]
    first = _dot_tn(p_prev[...], vlo_ref[...])

    def gate_act(part):
        hid = hids[part]
        act = (0.5 * live) * hid * (1.0 + lax.erf(hid * (2.0 ** -0.5)))
        out = []
        for ii in range(half // PEER_KEYS):
            i_row = part * (half // PEER_KEYS) + ii
            wsum = None
            for h in range(n_heads):
                s1 = s1_ref[i_row, h:h + 1, :]
                sel = (s1 + s2_ref[h]) >= st_ref[h, 0:1, :]
                g = jnp.where(sel, jnp.exp(s1 - st_ref[h, 1:2, :]) * e2_s[h], 0.0)
                wsum = g if wsum is None else wsum + g
            out.append((wsum * act[ii * PEER_KEYS:(ii + 1) * PEER_KEYS, :]).astype(BF16))
        return jnp.concatenate(out, axis=0)

    pa = gate_act(0)
    second = _dot_tn(pa, vhi_ref[...])
    p_prev[...] = gate_act(1)
    o_ref[...] += first + second


def peer_dense(hn, eu, ev, s1, s2, st, tm, half):
    r, d = hn.shape
    n_exp = eu.shape[0]
    n_heads = s2.shape[0]
    n_pairs = n_exp // (2 * half)
    n_half = n_exp // half
    kern = functools.partial(_peer_dense_kernel, n_heads=n_heads, half=half, n_trips=n_pairs + 1)
    once = pl.Buffered(1)
    return pl.pallas_call(
        kern, grid=(pl.cdiv(r, tm), n_pairs + 1),
        in_specs=[pl.BlockSpec((tm, d), lambda i, k: (i, 0), pipeline_mode=once),
                  pl.BlockSpec((2 * half, d), lambda i, k: (jnp.minimum(k, n_pairs - 1), 0)),
                  pl.BlockSpec((half, d), lambda i, k: (jnp.maximum(2 * k - 1, 0), 0)),
                  pl.BlockSpec((half, d), lambda i, k: (jnp.minimum(2 * k, n_half - 1), 0)),
                  pl.BlockSpec((2 * half // PEER_KEYS, n_heads, tm),
                               lambda i, k: (jnp.minimum(k, n_pairs - 1), 0, i)),
                  pl.BlockSpec((n_heads, PEER_KEYS, tm), lambda i, k: (0, 0, i), pipeline_mode=once),
                  pl.BlockSpec((n_heads, 8, tm), lambda i, k: (0, 0, i), pipeline_mode=once)],
        out_specs=pl.BlockSpec((tm, d), lambda i, k: (i, 0), pipeline_mode=once),
        out_shape=jax.ShapeDtypeStruct((r, d), F32),
        scratch_shapes=[pltpu.VMEM((half, tm), BF16), pltpu.VMEM((n_heads, PEER_KEYS, tm), F32)],
        compiler_params=_cparams(("parallel", "arbitrary"), 56), name="peer_dense")(
            hn, eu, ev, ev, s1, s2, st)


def kernel(x_prompt, x_sample, cache_k, cache_v, state_ssm, state_conv, page_table, norm_mix, w_in,
           conv_w, conv_b, dt_bias, a_log, d_skip, ssm_norm_w, sb_bias, w_branch_a, w_branch_b,
           w_out, norm_ffn, w_query, sub_keys, expert_u, expert_v, norm_final):
    depth = w_in.shape[0]
    assert depth == 1 and x_prompt.shape[0] == 1
    _, t, d = x_prompt.shape
    n_seq, seq, _ = x_sample.shape
    n_tok = n_seq * seq
    r = t + n_tok
    n_ssm_heads = dt_bias.shape[1]
    d_inner = n_ssm_heads * HEAD_P
    g = n_ssm_heads // GROUP_HEADS
    conv_dim = conv_w.shape[2]
    n_sb = sb_bias.shape[1]
    sbw = n_sb * SB_DH
    assert t % 256 == 0 and n_tok == 128 and r % 640 == 0

    wt = jnp.transpose(w_in[0])
    dt0 = d_inner + conv_dim
    w_rest = wt[dt0 + n_ssm_heads:].astype(BF16)
    w_dt = jnp.pad(wt[dt0:dt0 + n_ssm_heads], ((0, 128 - n_ssm_heads), (0, 0)))

    x_all = jnp.concatenate([x_prompt[0], x_sample.reshape(n_tok, d)], axis=0)
    xn = rmsnorm(x_all, norm_mix[0], BF16, tm=640)
    tm5 = r // 5
    (proj,) = matmul(xn, wt, [F32], tm5, 256, col0=0, n=dt0, w_rows=True)
    qscale = SB_DH ** -0.5 * LOG2E
    tm_p = max(c for c in range(128, 1025, 128) if t % c == 0)
    prm = dict(tm=tm_p, tn=512, row0=0, m=t, w_rows=True)
    smp = dict(tm=n_tok, tn=512, row0=t, m=n_tok, w_rows=True)
    (q_p,) = matmul(xn, w_rest, [BF16], col0=0, n=sbw, scale=qscale, **prm)
    (q_s,) = matmul(xn, w_rest, [BF16], col0=0, n=sbw, scale=qscale, **smp)
    kp_f, kp_bf = matmul(xn, w_rest, [F32, BF16], col0=sbw, n=sbw, **prm)
    (k_s,) = matmul(xn, w_rest, [F32], col0=sbw, n=sbw, **smp)
    vp_f, vp_bf = matmul(xn, w_rest, [F32, BF16], col0=2 * sbw, n=sbw, **prm)
    (v_s,) = matmul(xn, w_rest, [F32], col0=2 * sbw, n=sbw, **smp)
    (gates,) = matmul(xn, w_rest, [F32], tm5, 512, col0=3 * sbw, n=2 * d, w_rows=True)
    dt_raw = matmul(xn, w_dt, [F32], tm5, 128, w_rows=True)[0][:, :n_ssm_heads]

    def head_layouts(v):
        rows = v.shape[0]
        vg = v.reshape(rows, g, GROUP_HEADS).transpose(1, 0, 2)
        nat = jnp.pad(vg, ((0, 0), (0, 0), (0, 128 - GROUP_HEADS)))
        tr = jnp.pad(vg.transpose(0, 2, 1), ((0, 0), (0, 16 - GROUP_HEADS), (0, 0)))
        return nat, tr

    hp = jnp.stack([dt_bias[0], a_log[0], d_skip[0]], axis=0)
    hp_g = hp.reshape(3, g, GROUP_HEADS).transpose(1, 0, 2)
    hpn = jnp.pad(hp_g, ((0, 0), (0, 5), (0, 128 - GROUP_HEADS)))
    hpt = jnp.pad(hp_g.transpose(0, 2, 1), ((0, 0), (0, 16 - GROUP_HEADS), (0, 125)))
    nw = ssm_norm_w[0].reshape(1, d_inner)
    cw, cbias = conv_w[0], conv_b[0].reshape(1, conv_dim)

    dtn_p, dtt_p = head_layouts(dt_raw[:t])
    ya_head, ssm_p = ssd_prompt(proj, t, dtn_p, dtt_p, cw, cbias, hpn, hpt, nw, L=256)

    xbc_s = proj[t:, d_inner:d_inner + conv_dim].reshape(n_seq, seq, conv_dim)
    ext_s = jnp.concatenate([state_conv[0], xbc_s], axis=1)
    sh = jnp.stack([ext_s[:, k:k + seq].reshape(n_tok, conv_dim) for k in range(4)], axis=0)
    dtn_s, dtt_s = head_layouts(dt_raw[t:])
    ya, ssm_s = ssd_sample(sh, proj, t // n_tok, dtn_s, dtt_s, cw, cbias, hpn, hpt, nw,
                           state_ssm[0].reshape(n_seq, d_inner, D_STATE), seq, ya_head)

    yb_p = sb_prompt(q_p, kp_bf, vp_bf, t, sb_bias[0], blk=256, hb=8)
    n_phys, page = cache_k.shape[1], cache_k.shape[2]
    yb_s = sb_sample(q_s, k_s, v_s, cache_k.reshape(n_phys, page * n_sb, SB_DH),
                     cache_v.reshape(n_phys, page * n_sb, SB_DH), page_table, sb_bias[0], n_sb, seq)

    yb = jnp.concatenate([yb_p, yb_s.reshape(n_tok, sbw)], axis=0)
    merged = merge_branches(ya, yb, w_branch_a[0].astype(BF16), w_branch_b[0].astype(BF16),
                            gates, 0, d, tm=640, tn=512)
    h1 = matmul_resid(merged, w_out[0].astype(BF16), x_all, tm=640, tn=512)

    hn = rmsnorm(h1, norm_ffn[0], BF16, tm=640)
    (qp,) = matmul(hn, w_query[0].astype(BF16), [BF16], tm5, 512)
    s1_hk, s2, stats = peer_topk(qp, sub_keys[0], tm=128)
    s1 = s1_hk.transpose(1, 0, 2)
    peer = peer_dense(hn, expert_u[0].astype(BF16), expert_v[0].astype(BF16), s1, s2, stats,
                      tm=768 if r > 768 else r, half=256)
    y_p, y_s = rmsnorm_add_split(h1, peer, norm_final, t)

    y_prompt = y_p.reshape(1, t, d)
    y_sample = y_s.reshape(n_seq, seq, d)
    k_prompt = kp_f.reshape(1, 1, t, n_sb, SB_DH)
    v_prompt = vp_f.reshape(1, 1, t, n_sb, SB_DH)
    ssm_prompt = ssm_p.reshape(1, 1, n_ssm_heads, HEAD_P, D_STATE)
    conv_prompt = proj[t - 3:t, d_inner:d_inner + conv_dim].reshape(1, 1, 3, conv_dim)
    k_sample = k_s.reshape(1, n_seq, seq, n_sb, SB_DH)
    v_sample = v_s.reshape(1, n_seq, seq, n_sb, SB_DH)
    ssm_sample = ssm_s.reshape(1, n_seq, n_ssm_heads, HEAD_P, D_STATE)
    conv_sample = ext_s[:, seq:].reshape(1, n_seq, 3, conv_dim)
    return (y_prompt, y_sample, k_prompt, v_prompt, ssm_prompt, conv_prompt,
            k_sample, v_sample, ssm_sample, conv_sample)
```

```python
import functools
import math

import numpy as np
import jax
import jax.numpy as jnp
from jax import lax
from jax.experimental import pallas as pl
from jax.experimental.pallas import tpu as pltpu

F32 = jnp.float32
BF16 = jnp.bfloat16
EPS = 1e-6
NEG_BIG = -1e30

HEAD_P = 64
GROUP_HEADS = 8
GROUP_W = HEAD_P * GROUP_HEADS
D_STATE = 128
SB_DH = 128
LAG = 1
PEER_K = 16
PEER_KEYS = 128


def _cparams(sem, vmem_mb):
    return pltpu.CompilerParams(dimension_semantics=sem,
                                vmem_limit_bytes=vmem_mb * 1024 * 1024)


def _softplus(x):
    return jnp.maximum(x, 0.0) + jnp.log1p(jnp.exp(-jnp.abs(x)))


LOG2E = 1.4426950408889634


def _softplus2(x):
    return jnp.where(x > 64.0, x, jnp.log(1.0 + jnp.exp2(x)) * LOG2E)


def _silu(x):
    return x * jax.nn.sigmoid(x)


def _split2(x):
    hi = x.astype(BF16)
    lo = (x - hi.astype(F32)).astype(BF16)
    return hi, lo


def _split3(x):
    hi = x.astype(BF16)
    r = x - hi.astype(F32)
    mid = r.astype(BF16)
    lo = (r - mid.astype(F32)).astype(BF16)
    return hi, mid, lo


def _dot(a, b):
    return jnp.dot(a, b, preferred_element_type=F32)


def _dot_nt(a, b):
    return lax.dot_general(a, b, (((1,), (1,)), ((), ())), preferred_element_type=F32)


def _dot_tn(a, b):
    return lax.dot_general(a, b, (((0,), (0,)), ((), ())), preferred_element_type=F32)


def _rms_kernel(x_ref, w_ref, o_ref):
    x = x_ref[...]
    r = x * lax.rsqrt(jnp.mean(x * x, axis=-1, keepdims=True) + EPS)
    o_ref[...] = (r * w_ref[...]).astype(o_ref.dtype)


def rmsnorm(x, w, out_dtype, tm=128):
    m, d = x.shape
    row = pl.BlockSpec((tm, d), lambda i: (i, 0))
    return pl.pallas_call(
        _rms_kernel, grid=(m // tm,),
        in_specs=[row, pl.BlockSpec((1, d), lambda i: (0, 0))], out_specs=row,
        out_shape=jax.ShapeDtypeStruct((m, d), out_dtype),
        compiler_params=_cparams(("parallel",), 40), name="rmsnorm")(x, w.reshape(1, d))


def _rms_add_split_kernel(x_ref, y_ref, w_ref, head_ref, tail_ref, *, head_blocks):
    i = pl.program_id(0)
    x = x_ref[...] + y_ref[...]
    r = x * lax.rsqrt(jnp.mean(x * x, axis=-1, keepdims=True) + EPS)
    out = r * w_ref[...]

    @pl.when(i < head_blocks)
    def _():
        head_ref[...] = out

    @pl.when(i >= head_blocks)
    def _():
        tail_ref[...] = out


def rmsnorm_add_split(x, y, w, n_head, tm=128):
    m, d = x.shape
    hb = n_head // tm
    row = pl.BlockSpec((tm, d), lambda i: (i, 0))
    return pl.pallas_call(
        functools.partial(_rms_add_split_kernel, head_blocks=hb), grid=(m // tm,),
        in_specs=[row, row, pl.BlockSpec((1, d), lambda i: (0, 0))],
        out_specs=[pl.BlockSpec((tm, d), lambda i: (jnp.minimum(i, hb - 1), 0)),
                   pl.BlockSpec((tm, d), lambda i: (jnp.maximum(i - hb, 0), 0))],
        out_shape=[jax.ShapeDtypeStruct((n_head, d), F32),
                   jax.ShapeDtypeStruct((m - n_head, d), F32)],
        compiler_params=_cparams(("arbitrary",), 40), name="rmsnorm_out")(x, y, w.reshape(1, d))


def _mm_kernel(a_ref, w_ref, *o_refs, scale, w_rows):
    w = w_ref[...].astype(BF16)
    acc = _dot_nt(a_ref[...], w) if w_rows else _dot(a_ref[...], w)
    if scale is not None:
        acc = acc * scale
    for o_ref in o_refs:
        o_ref[...] = acc.astype(o_ref.dtype)


def matmul(a, w, out_dtypes, tm, tn, col0=0, n=None, scale=None, row0=0, m=None, w_rows=False,
           a_once=False):
    k = a.shape[1]
    m = a.shape[0] if m is None else m
    if n is None:
        n = w.shape[0] if w_rows else w.shape[1]
    c0, r0 = col0 // tn, row0 // tm
    out = pl.BlockSpec((tm, tn), lambda i, j: (i, j))
    wspec = (pl.BlockSpec((tn, k), lambda i, j: (c0 + j, 0)) if w_rows
             else pl.BlockSpec((k, tn), lambda i, j: (0, c0 + j)))
    return pl.pallas_call(
        functools.partial(_mm_kernel, scale=scale, w_rows=w_rows), grid=(m // tm, n // tn),
        in_specs=[pl.BlockSpec((tm, k), lambda i, j: (r0 + i, 0),
                               pipeline_mode=pl.Buffered(1) if a_once else None), wspec],
        out_specs=[out] * len(out_dtypes),
        out_shape=[jax.ShapeDtypeStruct((m, n), dt) for dt in out_dtypes],
        compiler_params=_cparams(("parallel", "arbitrary"), 52), name="matmul")(a, w)


def _mm_resid_kernel(a_ref, w_ref, x_ref, o_ref):
    o_ref[...] = x_ref[...] + _dot(a_ref[...], w_ref[...])


def matmul_resid(a, w, x, tm, tn):
    m, k = a.shape
    n = w.shape[1]
    return pl.pallas_call(
        _mm_resid_kernel, grid=(m // tm, n // tn),
        in_specs=[pl.BlockSpec((tm, k), lambda i, j: (i, 0)),
                  pl.BlockSpec((k, tn), lambda i, j: (0, j)),
                  pl.BlockSpec((tm, tn), lambda i, j: (i, j))],
        out_specs=pl.BlockSpec((tm, tn), lambda i, j: (i, j)),
        out_shape=jax.ShapeDtypeStruct((m, n), F32),
        compiler_params=_cparams(("parallel", "arbitrary"), 52), name="matmul_resid")(a, w, x)


def _mm_merge_kernel(ya_ref, yb_ref, wa_ref, wb_ref, ga_ref, gb_ref, o_ref):
    a = _dot(ya_ref[...], wa_ref[...])
    b = _dot(yb_ref[...], wb_ref[...])
    o_ref[...] = (jax.nn.sigmoid(ga_ref[...]) * a + jax.nn.sigmoid(gb_ref[...]) * b).astype(o_ref.dtype)


def merge_branches(ya, yb, wa, wb, proj, ga_col, gb_col, tm, tn):
    m, ka = ya.shape
    kb = yb.shape[1]
    n = wa.shape[1]
    ga0, gb0 = ga_col // tn, gb_col // tn
    return pl.pallas_call(
        _mm_merge_kernel, grid=(m // tm, n // tn),
        in_specs=[pl.BlockSpec((tm, ka), lambda i, j: (i, 0)),
                  pl.BlockSpec((tm, kb), lambda i, j: (i, 0)),
                  pl.BlockSpec((ka, tn), lambda i, j: (0, j)),
                  pl.BlockSpec((kb, tn), lambda i, j: (0, j)),
                  pl.BlockSpec((tm, tn), lambda i, j: (i, ga0 + j)),
                  pl.BlockSpec((tm, tn), lambda i, j: (i, gb0 + j))],
        out_specs=pl.BlockSpec((tm, tn), lambda i, j: (i, j)),
        out_shape=jax.ShapeDtypeStruct((m, n), BF16),
        compiler_params=_cparams(("parallel", "arbitrary"), 52), name="merge")(
            ya, yb, wa, wb, proj, proj)


def _conv_silu(ext, w_ref, b_ref, L):
    acc = b_ref[...] + w_ref[3:4, :] * ext[8:8 + L, :]
    acc = acc + w_ref[2:3, :] * ext[7:7 + L, :]
    acc = acc + w_ref[1:2, :] * ext[6:6 + L, :]
    acc = acc + w_ref[0:1, :] * ext[5:5 + L, :]
    return _silu(acc)


def _pair_terms(pr, x, dt, acs, acs_t, cb, ch, mask, hpn, lo_half):
    ra, rb = 2 * pr, 2 * pr + 1
    n_rows = acs.shape[0]
    cola = jnp.broadcast_to(acs[:, ra:ra + 1], (n_rows, 128))
    colb = jnp.broadcast_to(acs[:, rb:rb + 1], (n_rows, 128))
    reps = mask.shape[1] // 128
    rowa, rowb = acs_t[ra:ra + 1, :], acs_t[rb:rb + 1, :]
    ma = cb * jnp.exp(jnp.where(mask, jnp.tile(cola, (1, reps)) - rowa, NEG_BIG))
    mb = cb * jnp.exp(jnp.where(mask, jnp.tile(colb, (1, reps)) - rowb, NEG_BIG))
    mcat = jnp.concatenate([ma.astype(BF16), mb.astype(BF16)], axis=1)
    xp = x[:, pr * 128:(pr + 1) * 128]
    xdt = xp * jnp.where(lo_half, dt[:, ra:ra + 1], dt[:, rb:rb + 1])
    rhs = jnp.concatenate([jnp.where(lo_half, xdt, 0.0), jnp.where(lo_half, 0.0, xdt)],
                          axis=0).astype(BF16)
    yd = _dot(mcat, rhs)
    acs_pair = jnp.where(lo_half, cola, colb)
    dsk = jnp.where(lo_half[0:1, :], hpn[2:3, ra:ra + 1], hpn[2:3, rb:rb + 1])
    y = yd + ch[:, pr * 128:(pr + 1) * 128] * jnp.exp(acs_pair) + xp * dsk
    return y, xdt, acs_pair


def _ssd_prompt_kernel(xs_ref, b_ref, c_ref, z_ref, dtn_ref, dtt_ref,
                       cwx_ref, cwb_ref, cwc_ref, cbx_ref, cbb_ref, cbc_ref,
                       hpn_ref, hpt_ref, nw_ref, lin_ref, uin_ref,
                       y_ref, hout_ref, extx, extb, extc, hs, *, L):
    c = pl.program_id(1)

    @pl.when(c == 0)
    def _():
        extx[0:8, :] = jnp.zeros((8, GROUP_W), F32)
        extb[0:8, :] = jnp.zeros((8, D_STATE), F32)
        extc[0:8, :] = jnp.zeros((8, D_STATE), F32)
        hs[...] = jnp.zeros_like(hs)

    extx[8:8 + L, :] = xs_ref[...]
    extb[8:8 + L, :] = b_ref[...]
    extc[8:8 + L, :] = c_ref[...]
    x = _conv_silu(extx, cwx_ref, cbx_ref, L)
    bm = _conv_silu(extb, cwb_ref, cbb_ref, L).astype(BF16)
    cm = _conv_silu(extc, cwc_ref, cbc_ref, L).astype(BF16)
    extx[0:8, :] = extx[L:L + 8, :]
    extb[0:8, :] = extb[L:L + 8, :]
    extc[0:8, :] = extc[L:L + 8, :]

    hpn = hpn_ref[0]
    hpt = hpt_ref[0]
    dt = _softplus(dtn_ref[0] + hpn[0:1, :])
    d_a = dt * (-jnp.exp(hpn[1:2, :]))
    d_at = _softplus(dtt_ref[0] + hpt[:, 0:1]) * (-jnp.exp(hpt[:, 1:2]))
    lin = lin_ref[...]
    uin = uin_ref[...]
    acs = sum(_dot(lin, p) for p in _split3(d_a))
    acs_t = sum(_dot(p, uin) for p in _split3(d_at))

    rows = lax.broadcasted_iota(jnp.int32, (L, L), 0)
    cols = lax.broadcasted_iota(jnp.int32, (L, L), 1)
    causal = rows >= cols
    lo_half = lax.broadcasted_iota(jnp.int32, (L, 128), 1) < HEAD_P
    sub_lo = lax.broadcasted_iota(jnp.int32, (128, D_STATE), 0) < HEAD_P

    cb = _dot_nt(cm, bm)
    ch = _dot_nt(cm, hs[...].astype(BF16))

    ys = []
    for pr in range(GROUP_HEADS // 2):
        ra, rb = 2 * pr, 2 * pr + 1
        y, xdt, acs_pair = _pair_terms(pr, x, dt, acs, acs_t, cb, ch, causal, hpn, lo_half)
        ys.append(y)
        lasta = acs_t[ra:ra + 1, L - 1:L]
        lastb = acs_t[rb:rb + 1, L - 1:L]
        dend = jnp.exp(jnp.where(lo_half[0:1, :], lasta, lastb) - acs_pair)
        s_new = _dot_tn((xdt * dend).astype(BF16), bm)
        cd = jnp.where(sub_lo, jnp.exp(lasta), jnp.exp(lastb))
        sl = slice(pr * 128, (pr + 1) * 128)
        hs[sl, :] = hs[sl, :] * cd + s_new

    yg = jnp.concatenate(ys, axis=1)
    u = yg * _silu(z_ref[...])
    u = u * lax.rsqrt(jnp.mean(u * u, axis=-1, keepdims=True) + EPS)
    y_ref[...] = (u * nw_ref[...]).astype(y_ref.dtype)

    @pl.when(c == pl.num_programs(1) - 1)
    def _():
        hout_ref[...] = hs[...]


def _tri_incl(n):
    return np.tril(np.ones((n, n), np.float32))


def ssd_prompt(proj, t, dt_n, dt_t, conv_w, conv_b, hpn, hpt, norm_w, L):
    g = dt_n.shape[0]
    nc = t // L
    d_inner = g * GROUP_W
    xb = d_inner // GROUP_W
    bb = (2 * d_inner) // D_STATE
    cbk = bb + g
    lin = jnp.asarray(_tri_incl(L), BF16)
    uin = jnp.asarray(_tri_incl(L).T, BF16)
    const = lambda gi, ci: (0, 0)
    kern = functools.partial(_ssd_prompt_kernel, L=L)
    return pl.pallas_call(
        kern, grid=(g, nc),
        in_specs=[
            pl.BlockSpec((L, GROUP_W), lambda gi, ci: (ci, xb + gi)),
            pl.BlockSpec((L, D_STATE), lambda gi, ci: (ci, bb + gi)),
            pl.BlockSpec((L, D_STATE), lambda gi, ci: (ci, cbk + gi)),
            pl.BlockSpec((L, GROUP_W), lambda gi, ci: (ci, gi)),
            pl.BlockSpec((1, L, 128), lambda gi, ci: (gi, ci, 0)),
            pl.BlockSpec((1, 16, L), lambda gi, ci: (gi, 0, ci)),
            pl.BlockSpec((4, GROUP_W), lambda gi, ci: (0, gi)),
            pl.BlockSpec((4, D_STATE), lambda gi, ci: (0, bb - xb * 4 + gi)),
            pl.BlockSpec((4, D_STATE), lambda gi, ci: (0, bb - xb * 4 + g + gi)),
            pl.BlockSpec((1, GROUP_W), lambda gi, ci: (0, gi)),
            pl.BlockSpec((1, D_STATE), lambda gi, ci: (0, bb - xb * 4 + gi)),
            pl.BlockSpec((1, D_STATE), lambda gi, ci: (0, bb - xb * 4 + g + gi)),
            pl.BlockSpec((1, 8, 128), lambda gi, ci: (gi, 0, 0)),
            pl.BlockSpec((1, 16, 128), lambda gi, ci: (gi, 0, 0)),
            pl.BlockSpec((1, GROUP_W), lambda gi, ci: (0, gi)),
            pl.BlockSpec((L, L), const),
            pl.BlockSpec((L, L), const),
        ],
        out_specs=[pl.BlockSpec((L, GROUP_W), lambda gi, ci: (ci, gi)),
                   pl.BlockSpec((GROUP_W, D_STATE), lambda gi, ci: (gi, 0))],
        out_shape=[jax.ShapeDtypeStruct((proj.shape[0], d_inner), BF16),
                   jax.ShapeDtypeStruct((d_inner, D_STATE), F32)],
        scratch_shapes=[pltpu.VMEM((L + 8, GROUP_W), F32), pltpu.VMEM((L + 8, D_STATE), F32),
                        pltpu.VMEM((L + 8, D_STATE), F32), pltpu.VMEM((GROUP_W, D_STATE), F32)],
        compiler_params=_cparams(("arbitrary", "arbitrary"), 40), name="ssd_prompt")(
            proj, proj, proj, proj, dt_n, dt_t, conv_w, conv_w, conv_w, conv_b, conv_b, conv_b,
            hpn, hpt, norm_w, lin, uin)


def _ssd_sample_kernel(shx_ref, shb_ref, shc_ref, z_ref, dtn_ref, dtt_ref,
                       cwx_ref, cwb_ref, cwc_ref, cbx_ref, cbb_ref, cbc_ref,
                       hpn_ref, hpt_ref, nw_ref, lseg_ref, useg_ref, sseg_ref, st_ref, y_all_ref,
                       y_ref, hout_ref, yacc, cm_s, bm_s, ea_s, xwt_s, cdl_s, *, n_tok, seq):
    b = pl.program_id(1)

    @pl.when(b == 0)
    def _():
        def conv(sh_ref, w_ref, b_ref):
            acc = b_ref[...] + w_ref[0:1, :] * sh_ref[0]
            for k in range(1, 4):
                acc = acc + w_ref[k:k + 1, :] * sh_ref[k]
            return _silu(acc)

        x = conv(shx_ref, cwx_ref, cbx_ref)
        bm = conv(shb_ref, cwb_ref, cbb_ref).astype(BF16)
        cm = conv(shc_ref, cwc_ref, cbc_ref).astype(BF16)
        hpn = hpn_ref[0]
        hpt = hpt_ref[0]
        dt = _softplus(dtn_ref[0] + hpn[0:1, :])
        d_a = dt * (-jnp.exp(hpn[1:2, :]))
        d_at = _softplus(dtt_ref[0] + hpt[:, 0:1]) * (-jnp.exp(hpt[:, 1:2]))
        lseg = lseg_ref[...]
        useg = useg_ref[...]
        sseg = sseg_ref[...]
        pa, pat = _split3(d_a), _split3(d_at)
        acs = sum(_dot(lseg, p) for p in pa)
        tot = sum(_dot(sseg, p) for p in pa)
        acs_t = sum(_dot(p, useg) for p in pat)
        tot_t = sum(_dot(p, sseg) for p in pat)
        mask = lseg > 0
        lo_half = lax.broadcasted_iota(jnp.int32, (n_tok, 128), 1) < HEAD_P
        cb = _dot_nt(cm, bm)
        zero_ch = jnp.zeros((n_tok, GROUP_W), F32)
        ys, xws, eas = [], [], []
        for pr in range(GROUP_HEADS // 2):
            ra, rb = 2 * pr, 2 * pr + 1
            y, xdt, acs_pair = _pair_terms(pr, x, dt, acs, acs_t, cb, zero_ch, mask, hpn, lo_half)
            tot_pair = jnp.where(lo_half, tot[:, ra:ra + 1], tot[:, rb:rb + 1])
            ys.append(y)
            xws.append(xdt * jnp.exp(tot_pair - acs_pair))
            eas.append(jnp.exp(acs_pair))
        yacc[...] = jnp.concatenate(ys, axis=1)
        ea_s[...] = jnp.concatenate(eas, axis=1)
        xwt_s[...] = jnp.transpose(jnp.concatenate(xws, axis=1)).astype(BF16)
        cm_s[...] = cm
        bm_s[...] = bm
        cdl_s[...] = jnp.concatenate(
            [jnp.broadcast_to(tot_t[r:r + 1, :], (HEAD_P, n_tok)) for r in range(GROUP_HEADS)], axis=0)

    h0 = st_ref[0]
    lo = b * seq
    rid = lax.broadcasted_iota(jnp.int32, (n_tok, GROUP_W), 0)
    in_rows = jnp.abs(2 * (rid - lo) - (seq - 1)) < seq
    lid = lax.broadcasted_iota(jnp.int32, (GROUP_W, n_tok), 1)
    in_lanes = jnp.abs(2 * (lid - lo) - (seq - 1)) < seq
    ch = _dot_nt(cm_s[...], h0.astype(BF16))
    yacc[...] += jnp.where(in_rows, ch * ea_s[...], 0.0)
    s_new = _dot(jnp.where(in_lanes, xwt_s[...], jnp.zeros_like(xwt_s[...])), bm_s[...])
    cdcol = jnp.sum(jnp.where(lid == lo, cdl_s[...], 0.0), axis=-1, keepdims=True)
    hout_ref[0] = h0 * jnp.exp(cdcol) + s_new

    @pl.when(b == pl.num_programs(1) - 1)
    def _():
        u = yacc[...] * _silu(z_ref[...])
        u = u * lax.rsqrt(jnp.mean(u * u, axis=-1, keepdims=True) + EPS)
        y_ref[...] = (u * nw_ref[...]).astype(y_ref.dtype)


def ssd_sample(sh, proj, z_row_block, dt_n, dt_t, conv_w, conv_b, hpn, hpt, norm_w, state, seq,
               y_all):
    g = dt_n.shape[0]
    n_tok = sh.shape[1]
    n_seq = n_tok // seq
    d_inner = g * GROUP_W
    xb4 = d_inner // D_STATE
    sid = np.arange(n_tok) // seq
    same = (sid[:, None] == sid[None, :]).astype(np.float32)
    lseg = jnp.asarray(same * _tri_incl(n_tok), BF16)
    useg = jnp.asarray((same * _tri_incl(n_tok)).T, BF16)
    sseg = jnp.asarray(same, BF16)
    const = lambda gi, bi: (0, 0)
    kern = functools.partial(_ssd_sample_kernel, n_tok=n_tok, seq=seq)
    return pl.pallas_call(
        kern, grid=(g, n_seq),
        in_specs=[
            pl.BlockSpec((4, n_tok, GROUP_W), lambda gi, bi: (0, 0, gi)),
            pl.BlockSpec((4, n_tok, D_STATE), lambda gi, bi: (0, 0, xb4 + gi)),
            pl.BlockSpec((4, n_tok, D_STATE), lambda gi, bi: (0, 0, xb4 + g + gi)),
            pl.BlockSpec((n_tok, GROUP_W), lambda gi, bi: (z_row_block, gi)),
            pl.BlockSpec((1, n_tok, 128), lambda gi, bi: (gi, 0, 0)),
            pl.BlockSpec((1, 16, n_tok), lambda gi, bi: (gi, 0, 0)),
            pl.BlockSpec((4, GROUP_W), lambda gi, bi: (0, gi)),
            pl.BlockSpec((4, D_STATE), lambda gi, bi: (0, xb4 + gi)),
            pl.BlockSpec((4, D_STATE), lambda gi, bi: (0, xb4 + g + gi)),
            pl.BlockSpec((1, GROUP_W), lambda gi, bi: (0, gi)),
            pl.BlockSpec((1, D_STATE), lambda gi, bi: (0, xb4 + gi)),
            pl.BlockSpec((1, D_STATE), lambda gi, bi: (0, xb4 + g + gi)),
            pl.BlockSpec((1, 8, 128), lambda gi, bi: (gi, 0, 0)),
            pl.BlockSpec((1, 16, 128), lambda gi, bi: (gi, 0, 0)),
            pl.BlockSpec((1, GROUP_W), lambda gi, bi: (0, gi)),
            pl.BlockSpec((n_tok, n_tok), const),
            pl.BlockSpec((n_tok, n_tok), const),
            pl.BlockSpec((n_tok, n_tok), const),
            pl.BlockSpec((1, GROUP_W, D_STATE), lambda gi, bi: (bi, gi, 0)),
            pl.BlockSpec(memory_space=pl.ANY),
        ],
        out_specs=[pl.BlockSpec((n_tok, GROUP_W), lambda gi, bi: (z_row_block, gi)),
                   pl.BlockSpec((1, GROUP_W, D_STATE), lambda gi, bi: (bi, gi, 0))],
        out_shape=[jax.ShapeDtypeStruct(y_all.shape, BF16),
                   jax.ShapeDtypeStruct(state.shape, F32)],
        input_output_aliases={19: 0},
        scratch_shapes=[pltpu.VMEM((n_tok, GROUP_W), F32), pltpu.VMEM((n_tok, D_STATE), BF16),
                        pltpu.VMEM((n_tok, D_STATE), BF16), pltpu.VMEM((n_tok, GROUP_W), F32),
                        pltpu.VMEM((GROUP_W, n_tok), BF16), pltpu.VMEM((GROUP_W, n_tok), F32)],
        compiler_params=_cparams(("arbitrary", "arbitrary"), 40), name="ssd_sample")(
            sh, sh, sh, proj, dt_n, dt_t, conv_w, conv_w, conv_w, conv_b, conv_b, conv_b,
            hpn, hpt, norm_w, lseg, useg, sseg, state, y_all)


def _sb_prompt_kernel(bias_ref, q_ref, k_ref, v_ref, u_ref, o_ref, *, blk, hb):
    g = pl.program_id(0)
    i = pl.program_id(1)
    umat = u_ref[...]
    rows = lax.broadcasted_iota(jnp.int32, (blk, blk), 0)
    cols = lax.broadcasted_iota(jnp.int32, (blk, blk), 1)
    strict = cols < rows

    def block(j, carry, masked):
        start = pl.multiple_of(j * blk, blk)
        hsl = [slice(h * SB_DH, (h + 1) * SB_DH) for h in range(hb)]
        zs, sps, sufs = [None] * hb, [None] * hb, [None] * hb
        accs, runs = [None] * hb, [None] * hb
        for slot in range(hb + 2 * LAG):
            h = slot
            if h < hb:
                zs[h] = (_dot_nt(q_ref[:, hsl[h]], k_ref[pl.ds(start, blk), hsl[h]])
                         + bias_ref[g * hb + h] * LOG2E)
                sp = _softplus2(zs[h])
                sps[h] = jnp.where(strict, sp, 0.0) if masked else sp
            h = slot - LAG
            if 0 <= h < hb:
                sufs[h] = _dot(sps[h].astype(BF16), umat)
            h = slot - 2 * LAG
            if 0 <= h < hb:
                w = jnp.exp2(zs[h] - sps[h] - sufs[h] - carry[h][1])
                if masked:
                    w = jnp.where(strict, w, 0.0)
                accs[h] = carry[h][0] + _dot(w.astype(BF16), v_ref[pl.ds(start, blk), hsl[h]])
                runs[h] = carry[h][1] + jnp.sum(sps[h], axis=-1, keepdims=True)
        return tuple(zip(accs, runs))

    carry = tuple((jnp.zeros((blk, SB_DH), F32), jnp.zeros((blk, 1), F32)) for _ in range(hb))
    carry = block(i, carry, True)
    carry = lax.fori_loop(0, i, lambda t, c: block(i - 1 - t, c, False), carry)
    for h in range(hb):
        o_ref[:, h * SB_DH:(h + 1) * SB_DH] = carry[h][0].astype(o_ref.dtype)


def _tri_later(n):
    return np.tril(np.ones((n, n), np.float32), -1)


def sb_prompt(q, k, v, t, bias, blk, hb):
    width = q.shape[1]
    gw = hb * SB_DH
    umat = jnp.asarray(_tri_later(blk), BF16)
    kern = functools.partial(_sb_prompt_kernel, blk=blk, hb=hb)
    return pl.pallas_call(
        kern, grid=(width // gw, t // blk),
        in_specs=[
            pl.BlockSpec(memory_space=pltpu.SMEM),
            pl.BlockSpec((blk, gw), lambda g, i: (i, g)),
            pl.BlockSpec((t, gw), lambda g, i: (0, g), pipeline_mode=pl.Buffered(1)),
            pl.BlockSpec((t, gw), lambda g, i: (0, g), pipeline_mode=pl.Buffered(1)),
            pl.BlockSpec((blk, blk), lambda g, i: (0, 0)),
        ],
        out_specs=pl.BlockSpec((blk, gw), lambda g, i: (i, g)),
        out_shape=jax.ShapeDtypeStruct((t, width), BF16),
        compiler_params=_cparams(("arbitrary", "arbitrary"), 48), name="sb_prompt")(
            bias, q, k, v, umat)


def _sb_sample_kernel(pt_ref, qbd_ref, brow_ref, kn_ref, vn_ref, tn_ref, tp_ref, *rest,
                      n_heads, seq, page, ppb, grp):
    k_pages, v_pages = rest[:ppb], rest[ppb:2 * ppb]
    o_ref, acc, run = rest[2 * ppb:2 * ppb + 3]
    bufs = rest[2 * ppb + 3:]
    s = pl.program_id(1)
    qbd = qbd_ref[0]
    brow = brow_ref[...]
    nq = n_heads * 8

    def process(kb, vb, tmat, mask):
        z = _dot(kb, qbd) + brow
        sp = _softplus2(z)
        if mask is not None:
            sp = jnp.where(mask, sp, 0.0)
        hi, lo = _split2(sp)
        suffix = _dot(tmat, hi) + _dot(tmat, lo)
        w = jnp.exp2(z - sp - suffix - run[...])
        if mask is not None:
            w = jnp.where(mask, w, 0.0)
        acc[...] += _dot_tn(w.astype(BF16), vb)
        run[...] += jnp.sum(sp, axis=0, keepdims=True)

    @pl.when(s == 0)
    def _():
        acc[...] = jnp.zeros_like(acc)
        run[...] = jnp.zeros_like(run)
        fill = jnp.zeros((page - 8, n_heads * SB_DH), F32)
        knew = jnp.concatenate([kn_ref[0], fill], axis=0).astype(BF16)
        vnew = jnp.concatenate([vn_ref[0], fill], axis=0).astype(BF16)
        key = lax.broadcasted_iota(jnp.int32, (page, nq), 0)
        qpos = lax.broadcasted_iota(jnp.int32, (page, nq), 1) % 8
        process(knew, vnew, tn_ref[...], key < qpos)

    for gi in range(ppb // grp):
        kc, vc = bufs[2 * gi], bufs[2 * gi + 1]
        for u in range(grp):
            kp, vp = k_pages[gi * grp + u], v_pages[gi * grp + u]
            r0 = (grp - 1 - u) * page
            for h in range(n_heads):
                hs = slice(h * SB_DH, (h + 1) * SB_DH)
                kc[r0:r0 + page, hs] = kp[0, pl.ds(h, page, stride=n_heads), :].astype(BF16)
                vc[r0:r0 + page, hs] = vp[0, pl.ds(h, page, stride=n_heads), :].astype(BF16)
        process(kc[...], vc[...], tp_ref[...], None)

    @pl.when(s == pl.num_programs(1) - 1)
    def _():
        for h in range(n_heads):
            o_ref[0, :, h * SB_DH:(h + 1) * SB_DH] = (
                acc[h * 8:h * 8 + seq, h * SB_DH:(h + 1) * SB_DH].astype(o_ref.dtype))


def sb_sample(q_s, k_s, v_s, cache_k, cache_v, page_table, bias, n_heads, seq):
    n_seq, n_pages = page_table.shape
    page = cache_k.shape[1] // n_heads
    width = n_heads * SB_DH
    nq = n_heads * 8
    grp = 4
    ppb = 8 if n_pages % 8 == 0 else 4
    steps = n_pages // ppb
    q4 = jnp.pad(q_s.reshape(n_seq, seq, n_heads, SB_DH), ((0, 0), (0, 8 - seq), (0, 0), (0, 0)))
    eye = jnp.eye(n_heads, dtype=q_s.dtype)
    qbd = jnp.einsum("bthd,hg->bhdgt", q4, eye).reshape(n_seq, width, nq).astype(BF16)
    brow = jnp.repeat(bias * LOG2E, 8).reshape(1, nq)
    pad = lambda a: jnp.pad(a.reshape(n_seq, seq, width), ((0, 0), (0, 8 - seq), (0, 0)))
    tn = jnp.asarray(_tri_later(page).T, BF16)
    tp = jnp.asarray(_tri_later(grp * page).T, BF16)

    def page_spec(u):
        return pl.BlockSpec((1, page * n_heads, SB_DH),
                            lambda b, s, pt: (pt[b, n_pages - 1 - (ppb * s + u)], 0, 0))

    kern = functools.partial(_sb_sample_kernel, n_heads=n_heads, seq=seq, page=page, ppb=ppb,
                             grp=grp)
    grid_spec = pltpu.PrefetchScalarGridSpec(
        num_scalar_prefetch=1, grid=(n_seq, steps),
        in_specs=[
            pl.BlockSpec((1, width, nq), lambda b, s, pt: (b, 0, 0)),
            pl.BlockSpec((1, nq), lambda b, s, pt: (0, 0)),
            pl.BlockSpec((1, 8, width), lambda b, s, pt: (b, 0, 0)),
            pl.BlockSpec((1, 8, width), lambda b, s, pt: (b, 0, 0)),
            pl.BlockSpec((page, page), lambda b, s, pt: (0, 0)),
            pl.BlockSpec((grp * page, grp * page), lambda b, s, pt: (0, 0)),
        ] + [page_spec(u) for u in range(ppb)] + [page_spec(u) for u in range(ppb)],
        out_specs=pl.BlockSpec((1, seq, width), lambda b, s, pt: (b, 0, 0)),
        scratch_shapes=[pltpu.VMEM((nq, width), F32), pltpu.VMEM((1, nq), F32)]
        + [pltpu.VMEM((grp * page, width), BF16)] * (2 * (ppb // grp)))
    return pl.pallas_call(
        kern, grid_spec=grid_spec,
        out_shape=jax.ShapeDtypeStruct((n_seq, seq, width), BF16),
        compiler_params=_cparams(("arbitrary", "arbitrary"), 52), name="sb_sample")(
            page_table, qbd, brow, pad(k_s), pad(v_s), tn, tp,
            *([cache_k] * ppb), *([cache_v] * ppb))


def _top16_rows(s):
    n = s.shape[0]
    rid = lax.broadcasted_iota(jnp.int32, s.shape, 0)
    out = []
    for _ in range(PEER_K):
        m = jnp.max(s, axis=0, keepdims=True)
        first = jnp.min(jnp.where(s == m, rid, n), axis=0, keepdims=True)
        out.append(m)
        s = jnp.where(rid == first, -jnp.inf, s)
    return out


def _cmpx(v, i, j):
    a, b = v[i], v[j]
    v[i], v[j] = jnp.maximum(a, b), jnp.minimum(a, b)


def _bitonic_merge_desc(v):
    j = len(v) // 2
    while j >= 1:
        for i in range(len(v)):
            if i ^ j > i:
                _cmpx(v, i, i ^ j)
        j //= 2


def _bitonic_sort_desc(v):
    k = 2
    while k <= len(v):
        j = k // 2
        while j >= 1:
            for i in range(len(v)):
                if i ^ j > i:
                    _cmpx(v, *((i, i ^ j) if (i & k) == 0 else (i ^ j, i)))
            j //= 2
        k *= 2


def _top16_sorted(s):
    v = [s[8 * r:8 * r + 8, :] for r in range(PEER_K)]
    _bitonic_sort_desc(v)
    for shift in (4, 2, 1):
        w = [pltpu.roll(x, shift, 0) for x in v]
        v = [jnp.maximum(v[i], w[PEER_K - 1 - i]) for i in range(PEER_K)]
        _bitonic_merge_desc(v)
    return [x[0:1, :] for x in v]


def _peer_topk_kernel(q_ref, keys_ref, s1_ref, s2_ref, st_ref, *, n_heads):
    tm = q_ref.shape[0]
    for h in range(n_heads):
        tops = []
        for c in range(2):
            qh = q_ref[:, (2 * h + c) * 128:(2 * h + c + 1) * 128]
            sc = _dot_nt(keys_ref[h, c].astype(BF16), qh)
            (s1_ref, s2_ref)[c][h] = sc
            tops.append(_top16_sorted(sc))
        sv1 = jnp.concatenate(tops[0], axis=0)
        sv2 = jnp.concatenate(tops[1], axis=0)
        cand = jnp.concatenate(
            [tops[0][0] + sv2] + [tops[0][a] + sv2[0:8] for a in range(1, 8)]
            + [sv1[8:16] + tops[1][0]], axis=0)
        best = _top16_rows(cand)
        m = best[0]
        zsum = sum(jnp.exp(v - m) for v in best)
        max2 = tops[1][0]
        st_ref[h] = jnp.concatenate(
            [best[PEER_K - 1], m + jnp.log(zsum) - max2, max2, jnp.zeros((5, tm), F32)], axis=0)


def peer_topk(qp, sub_keys, tm):
    r = qp.shape[0]
    n_heads = sub_keys.shape[0]
    kern = functools.partial(_peer_topk_kernel, n_heads=n_heads)
    return pl.pallas_call(
        kern, grid=(r // tm,),
        in_specs=[pl.BlockSpec((tm, qp.shape[1]), lambda i: (i, 0)),
                  pl.BlockSpec(sub_keys.shape, lambda i: (0, 0, 0, 0))],
        out_specs=[pl.BlockSpec((n_heads, PEER_KEYS, tm), lambda i: (0, 0, i)),
                   pl.BlockSpec((n_heads, PEER_KEYS, tm), lambda i: (0, 0, i)),
                   pl.BlockSpec((n_heads, 8, tm), lambda i: (0, 0, i))],
        out_shape=[jax.ShapeDtypeStruct((n_heads, PEER_KEYS, r), F32),
                   jax.ShapeDtypeStruct((n_heads, PEER_KEYS, r), F32),
                   jax.ShapeDtypeStruct((n_heads, 8, r), F32)],
        compiler_params=_cparams(("parallel",), 40), name="peer_topk")(qp, sub_keys)


def _peer_dense_kernel(hn_ref, u_ref, vlo_ref, vhi_ref, s1_ref, s2_ref, st_ref, o_ref, p_prev, e2_s,
                       *, n_heads, half, n_trips):
    k = pl.program_id(1)

    @pl.when(k == 0)
    def _():
        o_ref[...] = jnp.zeros_like(o_ref)
        p_prev[...] = jnp.zeros_like(p_prev)
        for h in range(n_heads):
            e2_s[h] = jnp.exp(s2_ref[h] - st_ref[h, 2:3, :])

    live = (k < n_trips - 1).astype(F32)
    hn = hn_ref[...]

    hids = [_dot_nt(u_ref[part * half:(part + 1) * half, :], hn) for part in range(2)]
    first = _dot_tn(p_prev[...], vlo_ref[...])

    def gate_act(part):
        hid = hids[part]
        act = (0.5 * live) * hid * (1.0 + lax.erf(hid * (2.0 ** -0.5)))
        out = []
        for ii in range(half // PEER_KEYS):
            i_row = part * (half // PEER_KEYS) + ii
            wsum = None
            for h in range(n_heads):
                s1 = s1_ref[i_row, h:h + 1, :]
                sel = (s1 + s2_ref[h]) >= st_ref[h, 0:1, :]
                g = jnp.where(sel, jnp.exp(s1 - st_ref[h, 1:2, :]) * e2_s[h], 0.0)
                wsum = g if wsum is None else wsum + g
            out.append((wsum * act[ii * PEER_KEYS:(ii + 1) * PEER_KEYS, :]).astype(BF16))
        return jnp.concatenate(out, axis=0)

    pa = gate_act(0)
    second = _dot_tn(pa, vhi_ref[...])
    p_prev[...] = gate_act(1)
    o_ref[...] += first + second


def peer_dense(hn, eu, ev, s1, s2, st, tm, half):
    r, d = hn.shape
    n_exp = eu.shape[0]
    n_heads = s2.shape[0]
    n_pairs = n_exp // (2 * half)
    n_half = n_exp // half
    kern = functools.partial(_peer_dense_kernel, n_heads=n_heads, half=half, n_trips=n_pairs + 1)
    once = pl.Buffered(1)
    return pl.pallas_call(
        kern, grid=(pl.cdiv(r, tm), n_pairs + 1),
        in_specs=[pl.BlockSpec((tm, d), lambda i, k: (i, 0), pipeline_mode=once),
                  pl.BlockSpec((2 * half, d), lambda i, k: (jnp.minimum(k, n_pairs - 1), 0)),
                  pl.BlockSpec((half, d), lambda i, k: (jnp.maximum(2 * k - 1, 0), 0)),
                  pl.BlockSpec((half, d), lambda i, k: (jnp.minimum(2 * k, n_half - 1), 0)),
                  pl.BlockSpec((2 * half // PEER_KEYS, n_heads, tm),
                               lambda i, k: (jnp.minimum(k, n_pairs - 1), 0, i)),
                  pl.BlockSpec((n_heads, PEER_KEYS, tm), lambda i, k: (0, 0, i), pipeline_mode=once),
                  pl.BlockSpec((n_heads, 8, tm), lambda i, k: (0, 0, i), pipeline_mode=once)],
        out_specs=pl.BlockSpec((tm, d), lambda i, k: (i, 0), pipeline_mode=once),
        out_shape=jax.ShapeDtypeStruct((r, d), F32),
        scratch_shapes=[pltpu.VMEM((half, tm), BF16), pltpu.VMEM((n_heads, PEER_KEYS, tm), F32)],
        compiler_params=_cparams(("parallel", "arbitrary"), 56), name="peer_dense")(
            hn, eu, ev, ev, s1, s2, st)


def kernel(x_prompt, x_sample, cache_k, cache_v, state_ssm, state_conv, page_table, norm_mix, w_in,
           conv_w, conv_b, dt_bias, a_log, d_skip, ssm_norm_w, sb_bias, w_branch_a, w_branch_b,
           w_out, norm_ffn, w_query, sub_keys, expert_u, expert_v, norm_final):
    depth = w_in.shape[0]
    assert depth == 1 and x_prompt.shape[0] == 1
    _, t, d = x_prompt.shape
    n_seq, seq, _ = x_sample.shape
    n_tok = n_seq * seq
    r = t + n_tok
    n_ssm_heads = dt_bias.shape[1]
    d_inner = n_ssm_heads * HEAD_P
    g = n_ssm_heads // GROUP_HEADS
    conv_dim = conv_w.shape[2]
    n_sb = sb_bias.shape[1]
    sbw = n_sb * SB_DH
    assert t % 256 == 0 and n_tok == 128 and r % 640 == 0

    wt = jnp.transpose(w_in[0])
    dt0 = d_inner + conv_dim
    w_rest = wt[dt0 + n_ssm_heads:].astype(BF16)
    w_dt = jnp.pad(wt[dt0:dt0 + n_ssm_heads], ((0, 128 - n_ssm_heads), (0, 0)))

    x_all = jnp.concatenate([x_prompt[0], x_sample.reshape(n_tok, d)], axis=0)
    xn = rmsnorm(x_all, norm_mix[0], BF16, tm=640)
    tm5 = r // 5
    (proj,) = matmul(xn, wt, [F32], tm5, 256, col0=0, n=dt0, w_rows=True)
    qscale = SB_DH ** -0.5 * LOG2E
    tm_p = max(c for c in range(128, 2049, 128) if t % c == 0)
    prm = dict(tm=tm_p, tn=512, row0=0, m=t, w_rows=True, a_once=True)
    smp = dict(tm=n_tok, tn=512, row0=t, m=n_tok, w_rows=True)
    (q_p,) = matmul(xn, w_rest, [BF16], col0=0, n=sbw, scale=qscale, **prm)
    (q_s,) = matmul(xn, w_rest, [BF16], col0=0, n=sbw, scale=qscale, **smp)
    kp_f, kp_bf = matmul(xn, w_rest, [F32, BF16], col0=sbw, n=sbw, **prm)
    (k_s,) = matmul(xn, w_rest, [F32], col0=sbw, n=sbw, **smp)
    vp_f, vp_bf = matmul(xn, w_rest, [F32, BF16], col0=2 * sbw, n=sbw, **prm)
    (v_s,) = matmul(xn, w_rest, [F32], col0=2 * sbw, n=sbw, **smp)
    (gates,) = matmul(xn, w_rest, [F32], tm5, 512, col0=3 * sbw, n=2 * d, w_rows=True)
    dt_raw = matmul(xn, w_dt, [F32], tm5, 128, w_rows=True)[0][:, :n_ssm_heads]

    def head_layouts(v):
        rows = v.shape[0]
        vg = v.reshape(rows, g, GROUP_HEADS).transpose(1, 0, 2)
        nat = jnp.pad(vg, ((0, 0), (0, 0), (0, 128 - GROUP_HEADS)))
        tr = jnp.pad(vg.transpose(0, 2, 1), ((0, 0), (0, 16 - GROUP_HEADS), (0, 0)))
        return nat, tr

    hp = jnp.stack([dt_bias[0], a_log[0], d_skip[0]], axis=0)
    hp_g = hp.reshape(3, g, GROUP_HEADS).transpose(1, 0, 2)
    hpn = jnp.pad(hp_g, ((0, 0), (0, 5), (0, 128 - GROUP_HEADS)))
    hpt = jnp.pad(hp_g.transpose(0, 2, 1), ((0, 0), (0, 16 - GROUP_HEADS), (0, 125)))
    nw = ssm_norm_w[0].reshape(1, d_inner)
    cw, cbias = conv_w[0], conv_b[0].reshape(1, conv_dim)

    dtn_p, dtt_p = head_layouts(dt_raw[:t])
    ya_head, ssm_p = ssd_prompt(proj, t, dtn_p, dtt_p, cw, cbias, hpn, hpt, nw, L=256)

    xbc_s = proj[t:, d_inner:d_inner + conv_dim].reshape(n_seq, seq, conv_dim)
    ext_s = jnp.concatenate([state_conv[0], xbc_s], axis=1)
    sh = jnp.stack([ext_s[:, k:k + seq].reshape(n_tok, conv_dim) for k in range(4)], axis=0)
    dtn_s, dtt_s = head_layouts(dt_raw[t:])
    ya, ssm_s = ssd_sample(sh, proj, t // n_tok, dtn_s, dtt_s, cw, cbias, hpn, hpt, nw,
                           state_ssm[0].reshape(n_seq, d_inner, D_STATE), seq, ya_head)

    yb_p = sb_prompt(q_p, kp_bf, vp_bf, t, sb_bias[0], blk=512 if t % 512 == 0 else 256, hb=4)
    n_phys, page = cache_k.shape[1], cache_k.shape[2]
    yb_s = sb_sample(q_s, k_s, v_s, cache_k.reshape(n_phys, page * n_sb, SB_DH),
                     cache_v.reshape(n_phys, page * n_sb, SB_DH), page_table, sb_bias[0], n_sb, seq)

    yb = jnp.concatenate([yb_p, yb_s.reshape(n_tok, sbw)], axis=0)
    merged = merge_branches(ya, yb, w_branch_a[0].astype(BF16), w_branch_b[0].astype(BF16),
                            gates, 0, d, tm=640, tn=512)
    h1 = matmul_resid(merged, w_out[0].astype(BF16), x_all, tm=640, tn=512)

    hn = rmsnorm(h1, norm_ffn[0], BF16, tm=640)
    (qp,) = matmul(hn, w_query[0].astype(BF16), [BF16], tm5, 512)
    s1_hk, s2, stats = peer_topk(qp, sub_keys[0], tm=128)
    s1 = s1_hk.transpose(1, 0, 2)
    peer = peer_dense(hn, expert_u[0].astype(BF16), expert_v[0].astype(BF16), s1, s2, stats,
                      tm=768 if r > 768 else r, half=256)
    y_p, y_s = rmsnorm_add_split(h1, peer, norm_final, t)

    y_prompt = y_p.reshape(1, t, d)
    y_sample = y_s.reshape(n_seq, seq, d)
    k_prompt = kp_f.reshape(1, 1, t, n_sb, SB_DH)
    v_prompt = vp_f.reshape(1, 1, t, n_sb, SB_DH)
    ssm_prompt = ssm_p.reshape(1, 1, n_ssm_heads, HEAD_P, D_STATE)
    conv_prompt = proj[t - 3:t, d_inner:d_inner + conv_dim].reshape(1, 1, 3, conv_dim)
    k_sample = k_s.reshape(1, n_seq, seq, n_sb, SB_DH)
    v_sample = v_s.reshape(1, n_seq, seq, n_sb, SB_DH)
    ssm_sample = ssm_s.reshape(1, n_seq, n_ssm_heads, HEAD_P, D_STATE)
    conv_sample = ext_s[:, seq:].reshape(1, n_seq, 3, conv_dim)
    return (y_prompt, y_sample, k_prompt, v_prompt, ssm_prompt, conv_prompt,
            k_sample, v_sample, ssm_sample, conv_sample)
```
